```python
import math
import jax, jax.numpy as jnp
from jax import lax
import numpy as np

D_MODEL = 4096
BATCH = 2
SEQ = 8192
DEPTH = 1

SSD_EXPAND = 2
D_SSD = SSD_EXPAND * D_MODEL
SSD_HEAD_DIM = 64
SSD_HEADS = D_SSD // SSD_HEAD_DIM
SSD_GROUPS = 8
SSD_HEADS_PER_GROUP = SSD_HEADS // SSD_GROUPS
SSD_STATE = 128
SSD_CONV = 4
SSD_CHUNK = 128
D_XBC = D_SSD + 2 * SSD_GROUPS * SSD_STATE
D_CONF = D_MODEL
CONF_KERNEL = 31
N_BRANCH = 2
SPLITS = [D_SSD, D_SSD + D_XBC, D_SSD + D_XBC + SSD_HEADS,
          D_SSD + D_XBC + SSD_HEADS + 2 * D_CONF,
          D_SSD + D_XBC + SSD_HEADS + 2 * D_CONF + D_MODEL]
D_IN = D_SSD + D_XBC + SSD_HEADS + 2 * D_CONF + N_BRANCH * D_MODEL
MOE_GROUPS = 8
EXPERTS_PER_GROUP = 8
TOP_K_WITHIN = 2
D_EXPERT = 512
RMS_EPS = 1e-6
LN_EPS = 1e-5

kernel_name = "hybrid_ssd_conformer_hmoe_block"


def rms_norm(x, g):
    xf = x.astype(jnp.float32)
    y = xf * lax.rsqrt(jnp.mean(xf * xf, axis=-1, keepdims=True) + RMS_EPS)
    return (y * g.astype(jnp.float32)).astype(x.dtype)


def causal_depthwise_conv(x, w, b):
    k = w.shape[0]
    y = lax.conv_general_dilated(
        x, w[:, None, :].astype(x.dtype), window_strides=(1,), padding=[(k - 1, 0)],
        dimension_numbers=('NWC', 'WIO', 'NWC'), feature_group_count=x.shape[-1])
    return y + b.astype(x.dtype)


def ssd_chunk_scan(xdt, a, bm, cm):
    b, L = xdt.shape[:2]
    nc = L // SSD_CHUNK

    def chunks(t):
        return jnp.moveaxis(t.reshape(b, nc, SSD_CHUNK, *t.shape[2:]), 1, 0)

    causal = jnp.tril(jnp.ones((SSD_CHUNK, SSD_CHUNK), dtype=bool))[None, :, :, None, None]

    def step(state, inp):
        xc, ac, bc, cc = inp
        acum = jnp.cumsum(ac, axis=1)
        seg = acum[:, :, None] - acum[:, None, :]
        decay = jnp.exp(jnp.where(causal, seg, -jnp.inf))
        cb = jnp.einsum('bign,bjgn->bijg', cc, bc)
        y = jnp.einsum('bijgh,bjghp->bighp', decay * cb[..., None], xc)
        y = y + jnp.einsum('bign,bghpn->bighp', cc, state) * jnp.exp(acum)[..., None]
        to_end = jnp.exp(acum[:, -1:] - acum)
        state = (state * jnp.exp(acum[:, -1])[..., None, None]
                 + jnp.einsum('bjgn,bjgh,bjghp->bghpn', bc, to_end, xc))
        return state, y

    init = jnp.zeros((b, SSD_GROUPS, SSD_HEADS_PER_GROUP, SSD_HEAD_DIM, SSD_STATE), jnp.float32)
    _, y = lax.scan(step, init, (chunks(xdt), chunks(a), chunks(bm), chunks(cm)))
    return jnp.moveaxis(y, 0, 1).reshape(xdt.shape)


def ssd_branch(z, xbc, dt_raw, conv_w, conv_b, dt_bias, a_log, d_skip, norm_g, w_out):
    b, L, _ = z.shape
    xbc = jax.nn.silu(causal_depthwise_conv(xbc, conv_w, conv_b))
    xs, bm, cm = jnp.split(xbc, [D_SSD, D_SSD + SSD_GROUPS * SSD_STATE], axis=-1)
    dt = jax.nn.softplus(dt_raw.astype(jnp.float32) + dt_bias.astype(jnp.float32))
    a = -jnp.exp(a_log.astype(jnp.float32))
    xh = xs.astype(jnp.float32).reshape(b, L, SSD_GROUPS, SSD_HEADS_PER_GROUP, SSD_HEAD_DIM)
    dtg = dt.reshape(b, L, SSD_GROUPS, SSD_HEADS_PER_GROUP)
    ag = (dt * a).reshape(b, L, SSD_GROUPS, SSD_HEADS_PER_GROUP)
    y = ssd_chunk_scan(xh * dtg[..., None], ag,
                       bm.astype(jnp.float32).reshape(b, L, SSD_GROUPS, SSD_STATE),
                       cm.astype(jnp.float32).reshape(b, L, SSD_GROUPS, SSD_STATE))
    y = y + d_skip.astype(jnp.float32).reshape(SSD_GROUPS, SSD_HEADS_PER_GROUP)[..., None] * xh
    y = y.reshape(b, L, D_SSD) * jax.nn.silu(z.astype(jnp.float32))
    yg = y.reshape(b, L, SSD_GROUPS, D_SSD // SSD_GROUPS)
    yg = yg * lax.rsqrt(jnp.mean(yg * yg, axis=-1, keepdims=True) + RMS_EPS)
    y = (yg.reshape(b, L, D_SSD) * norm_g.astype(jnp.float32)).astype(z.dtype)
    return y @ w_out


def conformer_branch(glu_in, glu_b, dw_w, dw_b, ln_g, ln_b, w_out):
    u, g = jnp.split(glu_in + glu_b.astype(glu_in.dtype), 2, axis=-1)
    u = u * jax.nn.sigmoid(g)
    u = causal_depthwise_conv(u, dw_w, dw_b)
    uf = u.astype(jnp.float32)
    mu = jnp.mean(uf, axis=-1, keepdims=True)
    var = jnp.mean(jnp.square(uf - mu), axis=-1, keepdims=True)
    uf = (uf - mu) * lax.rsqrt(var + LN_EPS) * ln_g.astype(jnp.float32) + ln_b.astype(jnp.float32)
    u = jax.nn.silu(uf).astype(glu_in.dtype)
    return u @ w_out


def hierarchical_moe(h, r_grp, r_grp_b, r_exp, r_exp_b, w_gate, w_up, w_down):
    b, L, d = h.shape
    t = h.reshape(b * L, d)
    tf = t.astype(jnp.float32)
    g_prob = jax.nn.softmax(tf @ r_grp.astype(jnp.float32) + r_grp_b.astype(jnp.float32), axis=-1)
    p_top, g_idx = lax.top_k(g_prob, 1)
    g_onehot = jax.nn.one_hot(g_idx[:, 0], MOE_GROUPS, dtype=jnp.float32)
    e_logits = jnp.einsum('td,dge->tge', tf, r_exp.astype(jnp.float32)) + r_exp_b.astype(jnp.float32)
    sel = jnp.einsum('tg,tge->te', g_onehot, e_logits)
    v, e_idx = lax.top_k(sel, TOP_K_WITHIN)
    w = jax.nn.softmax(v, axis=-1)
    within = jnp.sum(jax.nn.one_hot(e_idx, EXPERTS_PER_GROUP, dtype=jnp.float32) * w[..., None], axis=1)
    combine = (g_onehot[:, :, None] * (p_top * within)[:, None, :]).astype(t.dtype)
    out = jnp.zeros_like(t)
    for gi in range(MOE_GROUPS):
        hid = (jax.nn.silu(jnp.einsum('td,edf->tef', t, w_gate[gi]))
               * jnp.einsum('td,edf->tef', t, w_up[gi]))
        out = out + jnp.einsum('tef,efd->td', hid * combine[:, gi, :, None], w_down[gi])
    return out.reshape(b, L, d)


def setup_inputs(seed: int = 0) -> dict:
    key = jax.random.key(seed)
    ks = jax.random.split(key, 24)
    f32 = jnp.float32

    def nrm(k, shape, fan_in):
        return jax.random.normal(k, shape, f32) * (fan_in ** -0.5)

    def gain(k, shape):
        return 1.0 + 0.02 * jax.random.normal(k, shape, f32)

    def small(k, shape):
        return 0.01 * jax.random.normal(k, shape, f32)

    dt0 = jnp.exp(jax.random.uniform(ks[4], (DEPTH, SSD_HEADS), f32, math.log(1e-3), math.log(1e-1)))
    dt_bias = dt0 + jnp.log(-jnp.expm1(-dt0))
    a_log = jnp.log(jax.random.uniform(ks[5], (DEPTH, SSD_HEADS), f32, 1.0, 16.0))
    return {
        "x": jax.random.normal(ks[0], (BATCH, SEQ, D_MODEL), f32),
        "norm_mix": gain(ks[1], (DEPTH, D_MODEL)),
        "w_in": nrm(ks[2], (DEPTH, D_MODEL, D_IN), D_MODEL),
        "ssd_conv_w": nrm(ks[3], (DEPTH, SSD_CONV, D_XBC), SSD_CONV),
        "ssd_conv_b": small(ks[6], (DEPTH, D_XBC)),
        "ssd_dt_bias": dt_bias,
        "ssd_a_log": a_log,
        "ssd_d": gain(ks[7], (DEPTH, SSD_HEADS)),
        "ssd_norm": gain(ks[8], (DEPTH, D_SSD)),
        "ssd_w_out": nrm(ks[9], (DEPTH, D_SSD, D_MODEL), D_SSD),
        "conf_glu_b": small(ks[10], (DEPTH, 2 * D_CONF)),
        "conf_dw_w": nrm(ks[11], (DEPTH, CONF_KERNEL, D_CONF), CONF_KERNEL),
        "conf_dw_b": small(ks[12], (DEPTH, D_CONF)),
        "conf_ln_g": gain(ks[13], (DEPTH, D_CONF)),
        "conf_ln_b": small(ks[14], (DEPTH, D_CONF)),
        "conf_w_out": nrm(ks[15], (DEPTH, D_CONF, D_MODEL), D_CONF),
        "w_out": nrm(ks[16], (DEPTH, D_MODEL, D_MODEL), D_MODEL),
        "norm_ffn": gain(ks[17], (DEPTH, D_MODEL)),
        "router_group": nrm(ks[18], (DEPTH, D_MODEL, MOE_GROUPS), D_MODEL),
        "router_group_b": small(ks[19], (DEPTH, MOE_GROUPS)),
        "router_expert": nrm(ks[20], (DEPTH, D_MODEL, MOE_GROUPS, EXPERTS_PER_GROUP), D_MODEL),
        "router_expert_b": small(ks[21], (DEPTH, MOE_GROUPS, EXPERTS_PER_GROUP)),
        "expert_w_gate": nrm(ks[22], (DEPTH, MOE_GROUPS, EXPERTS_PER_GROUP, D_MODEL, D_EXPERT), D_MODEL),
        "expert_w_up": nrm(jax.random.fold_in(ks[22], 1), (DEPTH, MOE_GROUPS, EXPERTS_PER_GROUP, D_MODEL, D_EXPERT), D_MODEL),
        "expert_w_down": nrm(ks[23], (DEPTH, MOE_GROUPS, EXPERTS_PER_GROUP, D_EXPERT, D_MODEL), D_EXPERT),
        "norm_final": gain(jax.random.fold_in(ks[1], 7), (D_MODEL,)),
    }


def reference(x, norm_mix, w_in, ssd_conv_w, ssd_conv_b, ssd_dt_bias, ssd_a_log, ssd_d,
              ssd_norm, ssd_w_out, conf_glu_b, conf_dw_w, conf_dw_b, conf_ln_g, conf_ln_b,
              conf_w_out, w_out, norm_ffn, router_group, router_group_b, router_expert,
              router_expert_b, expert_w_gate, expert_w_up, expert_w_down, norm_final):
    for layer in range(DEPTH):
        h = rms_norm(x, norm_mix[layer])
        proj = h @ w_in[layer]
        z, xbc, dt_raw, glu_in, gate_ssd, gate_conf = jnp.split(proj, SPLITS, axis=-1)
        y_ssd = ssd_branch(z, xbc, dt_raw, ssd_conv_w[layer], ssd_conv_b[layer], ssd_dt_bias[layer],
                           ssd_a_log[layer], ssd_d[layer], ssd_norm[layer], ssd_w_out[layer])
        y_conf = conformer_branch(glu_in, conf_glu_b[layer], conf_dw_w[layer], conf_dw_b[layer],
                                  conf_ln_g[layer], conf_ln_b[layer], conf_w_out[layer])
        mixed = jax.nn.sigmoid(gate_ssd) * y_ssd + jax.nn.sigmoid(gate_conf) * y_conf
        x = x + mixed @ w_out[layer]
        h = rms_norm(x, norm_ffn[layer])
        x = x + hierarchical_moe(h, router_group[layer], router_group_b[layer], router_expert[layer],
                                 router_expert_b[layer], expert_w_gate[layer], expert_w_up[layer],
                                 expert_w_down[layer])
    return rms_norm(x, norm_final)
```

```python
import functools

import jax
import jax.numpy as jnp
from jax import lax
from jax.experimental import pallas as pl
from jax.experimental.pallas import tpu as pltpu

SSD_GROUPS = 8
SSD_CHUNK = 128
RMS_EPS = 1e-6
LN_EPS = 1e-5
NEG_BIG = -1e30
ROUTER_LANES = 128
EXPERT_LANE0 = 64
VMEM_LIMIT_BYTES = 56 * 1024 * 1024

F32 = jnp.float32
BF16 = jnp.bfloat16


def _cparams(semantics):
    return pltpu.CompilerParams(dimension_semantics=semantics,
                                vmem_limit_bytes=VMEM_LIMIT_BYTES)


def _tile(n, pref, mult=128):
    if n <= pref:
        return n
    t = (pref // mult) * mult
    while t >= mult:
        if n % t == 0:
            return t
        t -= mult
    return n


def _sigmoid(x):
    return 1.0 / (1.0 + jnp.exp(-x))


def _silu(x):
    return x * _sigmoid(x)


def _softplus(x):
    return jnp.maximum(x, 0.0) + jnp.log(1.0 + jnp.exp(-jnp.abs(x)))


def _rmsnorm_kernel(x_ref, g_ref, o_ref):
    x = x_ref[...]
    ms = jnp.mean(x * x, axis=-1, keepdims=True)
    o_ref[...] = (x * lax.rsqrt(ms + RMS_EPS) * g_ref[...]).astype(o_ref.dtype)


def _rmsnorm(x, g, out_dtype):
    T, D = x.shape
    tm = _tile(T, 256, 8)
    return pl.pallas_call(
        _rmsnorm_kernel,
        grid=(T // tm,),
        in_specs=[pl.BlockSpec((tm, D), lambda i: (i, 0)),
                  pl.BlockSpec((1, D), lambda i: (0, 0))],
        out_specs=pl.BlockSpec((tm, D), lambda i: (i, 0)),
        out_shape=jax.ShapeDtypeStruct((T, D), out_dtype),
        compiler_params=_cparams(("parallel",)),
        name="rmsnorm",
    )(x, g.reshape(1, D))


def _proj_act_kernel(a_ref, w_ref, o_ref, *, act):
    acc = jnp.dot(a_ref[...], w_ref[...], preferred_element_type=F32)
    if act == "silu":
        acc = _silu(acc)
    elif act == "sigmoid":
        acc = _sigmoid(acc)
    o_ref[...] = acc.astype(o_ref.dtype)


def _proj_act(a, w, act, out_dtype, name):
    M, K = a.shape
    N = w.shape[1]
    tm, tn = _tile(M, 512, 8), _tile(N, 1024)
    return pl.pallas_call(
        functools.partial(_proj_act_kernel, act=act),
        grid=(N // tn, M // tm),
        in_specs=[pl.BlockSpec((tm, K), lambda j, i: (i, 0)),
                  pl.BlockSpec((K, tn), lambda j, i: (0, j))],
        out_specs=pl.BlockSpec((tm, tn), lambda j, i: (i, j)),
        out_shape=jax.ShapeDtypeStruct((M, N), out_dtype),
        compiler_params=_cparams(("parallel", "parallel")),
        name=name,
    )(a, w)


def _proj_glu_kernel(a_ref, wu_ref, wg_ref, bu_ref, bg_ref, o_ref):
    a = a_ref[...]
    u = jnp.dot(a, wu_ref[...], preferred_element_type=F32) + bu_ref[...]
    g = jnp.dot(a, wg_ref[...], preferred_element_type=F32) + bg_ref[...]
    o_ref[...] = (u * _sigmoid(g)).astype(o_ref.dtype)


def _proj_glu(a, wu, wg, bu, bg, out_dtype):
    M, K = a.shape
    N = wu.shape[1]
    tm, tn = _tile(M, 512, 8), _tile(N, 512)
    return pl.pallas_call(
        _proj_glu_kernel,
        grid=(N // tn, M // tm),
        in_specs=[pl.BlockSpec((tm, K), lambda j, i: (i, 0)),
                  pl.BlockSpec((K, tn), lambda j, i: (0, j)),
                  pl.BlockSpec((K, tn), lambda j, i: (0, j)),
                  pl.BlockSpec((1, tn), lambda j, i: (0, j)),
                  pl.BlockSpec((1, tn), lambda j, i: (0, j))],
        out_specs=pl.BlockSpec((tm, tn), lambda j, i: (i, j)),
        out_shape=jax.ShapeDtypeStruct((M, N), out_dtype),
        compiler_params=_cparams(("parallel", "parallel")),
        name="proj_glu",
    )(a, wu, wg, bu.reshape(1, N), bg.reshape(1, N))


def _proj_dt_kernel(a_ref, w_ref, b_ref, dt_ref, dtT_ref):
    acc = jnp.dot(a_ref[...], w_ref[...], preferred_element_type=F32) + b_ref[...]
    dt = _softplus(acc)
    dt_ref[...] = dt
    dtT_ref[...] = dt.T


def _proj_dt(a, w, b):
    M, K = a.shape
    H = w.shape[1]
    tm = _tile(M, 512, 128)
    return pl.pallas_call(
        _proj_dt_kernel,
        grid=(M // tm,),
        in_specs=[pl.BlockSpec((tm, K), lambda i: (i, 0)),
                  pl.BlockSpec((K, H), lambda i: (0, 0)),
                  pl.BlockSpec((1, H), lambda i: (0, 0))],
        out_specs=[pl.BlockSpec((tm, H), lambda i: (i, 0)),
                   pl.BlockSpec((H, tm), lambda i: (0, i))],
        out_shape=[jax.ShapeDtypeStruct((M, H), F32),
                   jax.ShapeDtypeStruct((H, M), F32)],
        compiler_params=_cparams(("parallel",)),
        name="proj_dt",
    )(a, w, b.reshape(1, H))


def _out_gate_kernel(*refs, has_prev):
    if has_prev:
        a_ref, w_ref, g_ref, p_ref, o_ref = refs
    else:
        a_ref, w_ref, g_ref, o_ref = refs
    acc = jnp.dot(a_ref[...], w_ref[...], preferred_element_type=F32)
    acc = acc * g_ref[...].astype(F32)
    if has_prev:
        acc = acc + p_ref[...].astype(F32)
    o_ref[...] = acc.astype(o_ref.dtype)


def _out_gate(a, w, gates, gate_col0, prev, out_dtype, name):
    M, K = a.shape
    N = w.shape[1]
    tm, tn = _tile(M, 512, 8), _tile(N, 512)
    goff = gate_col0 // tn
    in_specs = [pl.BlockSpec((tm, K), lambda j, i: (i, 0)),
                pl.BlockSpec((K, tn), lambda j, i: (0, j)),
                pl.BlockSpec((tm, tn), lambda j, i: (i, j + goff))]
    args = [a, w, gates]
    if prev is not None:
        in_specs.append(pl.BlockSpec((tm, tn), lambda j, i: (i, j)))
        args.append(prev)
    return pl.pallas_call(
        functools.partial(_out_gate_kernel, has_prev=prev is not None),
        grid=(N // tn, M // tm),
        in_specs=in_specs,
        out_specs=pl.BlockSpec((tm, tn), lambda j, i: (i, j)),
        out_shape=jax.ShapeDtypeStruct((M, N), out_dtype),
        compiler_params=_cparams(("parallel", "parallel")),
        name=name,
    )(*args)


def _out_res_kernel(a_ref, w_ref, r_ref, o_ref):
    acc = jnp.dot(a_ref[...], w_ref[...], preferred_element_type=F32)
    o_ref[...] = r_ref[...] + acc


def _out_res(a, w, res):
    M, K = a.shape
    N = w.shape[1]
    tm, tn = _tile(M, 512, 8), _tile(N, 1024)
    return pl.pallas_call(
        _out_res_kernel,
        grid=(N // tn, M // tm),
        in_specs=[pl.BlockSpec((tm, K), lambda j, i: (i, 0)),
                  pl.BlockSpec((K, tn), lambda j, i: (0, j)),
                  pl.BlockSpec((tm, tn), lambda j, i: (i, j))],
        out_specs=pl.BlockSpec((tm, tn), lambda j, i: (i, j)),
        out_shape=jax.ShapeDtypeStruct((M, N), F32),
        compiler_params=_cparams(("parallel", "parallel")),
        name="out_res",
    )(a, w, res)


def _split3(v):
    hi = v.astype(BF16)
    r1 = v - hi.astype(F32)
    mid = r1.astype(BF16)
    lo = (r1 - mid.astype(F32)).astype(BF16)
    return hi, mid, lo


def _ssd_kernel(xs_ref, b_ref, c_ref, wx_ref, wb_ref, wc_ref, bx_ref, bb_ref, bc_ref,
                dt_ref, dtT_ref, alog_ref, alogT_ref, z_ref, dskip_ref, ng_ref,
                o_ref, ext_ref, state_ref, y_ref, *, Q, Hg, P, N, K):
    W = Hg * P
    HALO = 8
    c = pl.program_id(2)

    @pl.when(c == 0)
    def _():
        ext_ref[0:HALO, :] = jnp.zeros((HALO, W + 2 * N), F32)
        state_ref[...] = jnp.zeros_like(state_ref)

    ext_ref[HALO:HALO + Q, 0:W] = xs_ref[...].astype(F32)
    ext_ref[HALO:HALO + Q, W:W + N] = b_ref[...].astype(F32)
    ext_ref[HALO:HALO + Q, W + N:W + 2 * N] = c_ref[...].astype(F32)

    def conv_silu(lo, hi, w_ref, bias_ref, wlo):
        acc = bias_ref[:, wlo:wlo + (hi - lo)]
        for k in range(K):
            r0 = HALO - (K - 1) + k
            acc = acc + w_ref[k:k + 1, wlo:wlo + (hi - lo)] * ext_ref[r0:r0 + Q, lo:hi]
        return _silu(acc)

    bm = conv_silu(W, W + N, wb_ref, bb_ref, 0)
    cm = conv_silu(W + N, W + 2 * N, wc_ref, bc_ref, 0)

    dt = dt_ref[...]
    dtT = dtT_ref[...]
    a = dt * (-jnp.exp(alog_ref[...]))
    aT = dtT * (-jnp.exp(alogT_ref[...]))

    row = lax.broadcasted_iota(jnp.int32, (Q, Q), 0)
    col = lax.broadcasted_iota(jnp.int32, (Q, Q), 1)
    causal = col <= row
    tril = jnp.where(causal, 1.0, 0.0).astype(BF16)
    triu = jnp.where(row <= col, 1.0, 0.0).astype(BF16)
    acum = jnp.dot(jnp.concatenate([tril, tril, tril], axis=1),
                   jnp.concatenate(_split3(a), axis=0), preferred_element_type=F32)
    acumT = jnp.dot(jnp.concatenate(_split3(aT), axis=1),
                    jnp.concatenate([triu, triu, triu], axis=0), preferred_element_type=F32)

    bm16 = bm.astype(BF16)
    cm16 = cm.astype(BF16)
    cb = lax.dot_general(cm16, bm16, (((1,), (1,)), ((), ())), preferred_element_type=F32)
    bT = bm.T

    lane = lax.broadcasted_iota(jnp.int32, (1, 2 * P), 1)
    lo_half = lane < P
    zero16 = jnp.zeros((), BF16)

    ssq = jnp.zeros((Q, 1), F32)
    for q in range(Hg // 2):
        l0 = q * 2 * P
        x_pair = conv_silu(l0, l0 + 2 * P, wx_ref, bx_ref, l0)
        x16 = x_pair.astype(BF16)
        rhs_x = jnp.concatenate([jnp.where(lo_half, x16, zero16),
                                 jnp.where(lo_half, zero16, x16)], axis=0)
        s_old = state_ref[q]
        s16 = s_old.astype(BF16)
        rhs_s = jnp.concatenate([jnp.where(lo_half, s16, zero16),
                                 jnp.where(lo_half, zero16, s16)], axis=0)
        l_parts, ec_parts, bw_parts, g_parts = [], [], [], []
        for hh in range(2):
            h = 2 * q + hh
            ai = jnp.broadcast_to(acum[:, h:h + 1], (Q, Q))
            aj = acumT[h:h + 1, :]
            dtj = dtT[h:h + 1, :]
            seg = jnp.where(causal, ai - aj, NEG_BIG)
            l_parts.append((jnp.exp(seg) * cb * dtj).astype(BF16))
            ec_parts.append((jnp.exp(ai) * cm).astype(BF16))
            a_last = acumT[h:h + 1, Q - 1:Q]
            w_end = jnp.exp(a_last - aj) * dtj
            bw_parts.append((bT * w_end).astype(BF16))
            g_parts.append(jnp.exp(a_last))
        lhs_y = jnp.concatenate(l_parts + ec_parts, axis=1)
        rhs_y = jnp.concatenate([rhs_x, rhs_s], axis=0)
        y = jnp.dot(lhs_y, rhs_y, preferred_element_type=F32)
        upd = jnp.dot(jnp.concatenate(bw_parts, axis=1), rhs_x,
                      preferred_element_type=F32)
        decay = jnp.where(lo_half, g_parts[0], g_parts[1])
        state_ref[q] = s_old * decay + upd

        y = y + dskip_ref[:, l0:l0 + 2 * P] * x_pair
        y = y * z_ref[:, l0:l0 + 2 * P].astype(F32)
        y_ref[:, l0:l0 + 2 * P] = y
        ssq = ssq + jnp.sum(y * y, axis=-1, keepdims=True)

    ext_ref[0:HALO, :] = ext_ref[Q:Q + HALO, :]
    rstd = lax.rsqrt(ssq * (1.0 / W) + RMS_EPS)
    o_ref[...] = (y_ref[...] * rstd * ng_ref[...]).astype(o_ref.dtype)


def _ssd(xbc, zs, dt, dtT, conv_w, conv_b, a_log, d_skip, norm_g, *, batch, seq, d_ssd):
    T = batch * seq
    G = SSD_GROUPS
    Q = SSD_CHUNK
    H = a_log.shape[0]
    Hg = H // G
    P = d_ssd // H
    W = Hg * P
    N = (xbc.shape[1] - d_ssd) // (2 * G)
    K = conv_w.shape[0]
    assert 2 * P == 128 and N == 128 and Q == 128 and Hg % 2 == 0 and W % 128 == 0
    assert seq % Q == 0 and K - 1 <= 8
    nc = seq // Q
    nb_x = d_ssd // N

    dt_g = dt.reshape(T, G, Hg).transpose(1, 0, 2)
    alog_g = a_log.reshape(G, 1, Hg)
    alogT = a_log.reshape(H, 1)
    dskip = jnp.repeat(d_skip, P).reshape(1, d_ssd)
    cb2 = conv_b.reshape(1, -1)

    rowblk = lambda b, g, c: b * nc + c
    kern = functools.partial(_ssd_kernel, Q=Q, Hg=Hg, P=P, N=N, K=K)
    return pl.pallas_call(
        kern,
        grid=(batch, G, nc),
        in_specs=[
            pl.BlockSpec((Q, W), lambda b, g, c: (rowblk(b, g, c), g)),
            pl.BlockSpec((Q, N), lambda b, g, c: (rowblk(b, g, c), nb_x + g)),
            pl.BlockSpec((Q, N), lambda b, g, c: (rowblk(b, g, c), nb_x + G + g)),
            pl.BlockSpec((K, W), lambda b, g, c: (0, g)),
            pl.BlockSpec((K, N), lambda b, g, c: (0, nb_x + g)),
            pl.BlockSpec((K, N), lambda b, g, c: (0, nb_x + G + g)),
            pl.BlockSpec((1, W), lambda b, g, c: (0, g)),
            pl.BlockSpec((1, N), lambda b, g, c: (0, nb_x + g)),
            pl.BlockSpec((1, N), lambda b, g, c: (0, nb_x + G + g)),
            pl.BlockSpec((None, Q, Hg), lambda b, g, c: (g, rowblk(b, g, c), 0)),
            pl.BlockSpec((Hg, Q), lambda b, g, c: (g, rowblk(b, g, c))),
            pl.BlockSpec((None, 1, Hg), lambda b, g, c: (g, 0, 0)),
            pl.BlockSpec((Hg, 1), lambda b, g, c: (g, 0)),
            pl.BlockSpec((Q, W), lambda b, g, c: (rowblk(b, g, c), g)),
            pl.BlockSpec((1, W), lambda b, g, c: (0, g)),
            pl.BlockSpec((1, W), lambda b, g, c: (0, g)),
        ],
        out_specs=pl.BlockSpec((Q, W), lambda b, g, c: (rowblk(b, g, c), g)),
        out_shape=jax.ShapeDtypeStruct((T, d_ssd), BF16),
        scratch_shapes=[pltpu.VMEM((Q + 8, W + 2 * N), F32),
                        pltpu.VMEM((Hg // 2, N, 2 * P), F32),
                        pltpu.VMEM((Q, W), F32)],
        compiler_params=_cparams(("parallel", "parallel", "arbitrary")),
        name="ssd_scan",
    )(xbc, xbc, xbc, conv_w, conv_w, conv_w, cb2, cb2, cb2,
      dt_g, dtT, alog_g, alogT, zs, dskip, norm_g.reshape(1, d_ssd))


def _conf_kernel(u_ref, w_ref, b_ref, g_ref, beta_ref, o_ref, ext_ref, conv_ref, *, TM, C, K, HALO):
    NCH = C // 128
    i = pl.program_id(1)

    @pl.when(i == 0)
    def _():
        ext_ref[:, 0:HALO, :] = jnp.zeros((NCH, HALO, 128), F32)

    for ch in range(NCH):
        ext_ref[ch, HALO:HALO + TM, :] = u_ref[:, ch * 128:(ch + 1) * 128].astype(F32)

    def chunk_body(ch, carry):
        s1, = carry
        acc = jnp.broadcast_to(b_ref[ch], (TM, 128))
        for k in range(K):
            r0 = HALO - (K - 1) + k
            acc = acc + w_ref[ch, k:k + 1, :] * ext_ref[ch, r0:r0 + TM, :]
        conv_ref[ch] = acc
        ext_ref[ch, 0:HALO, :] = ext_ref[ch, TM:TM + HALO, :]
        return (s1 + acc,)

    s1, = lax.fori_loop(0, NCH, chunk_body, (jnp.zeros((TM, 128), F32),))
    mu = jnp.sum(s1, axis=-1, keepdims=True) * (1.0 / C)

    def var_body(ch, s2):
        d = conv_ref[ch] - mu
        return s2 + d * d

    s2 = lax.fori_loop(0, NCH, var_body, jnp.zeros((TM, 128), F32))
    rstd = lax.rsqrt(jnp.sum(s2, axis=-1, keepdims=True) * (1.0 / C) + LN_EPS)
    for ch in range(NCH):
        v = (conv_ref[ch] - mu) * rstd * g_ref[ch] + beta_ref[ch]
        o_ref[:, ch * 128:(ch + 1) * 128] = _silu(v).astype(o_ref.dtype)


def _conformer(u, dw_w, dw_b, ln_g, ln_b, *, batch, seq):
    T, C = u.shape
    K = dw_w.shape[0]
    HALO = 32
    assert K - 1 <= HALO and C % 128 == 0
    TM = _tile(seq, 128, 32)
    nt = seq // TM
    NCH = C // 128
    w3 = dw_w.reshape(K, NCH, 128).transpose(1, 0, 2)
    chunked = lambda v: v.reshape(NCH, 1, 128)
    kern = functools.partial(_conf_kernel, TM=TM, C=C, K=K, HALO=HALO)
    return pl.pallas_call(
        kern,
        grid=(batch, nt),
        in_specs=[pl.BlockSpec((TM, C), lambda b, i: (b * nt + i, 0)),
                  pl.BlockSpec((NCH, K, 128), lambda b, i: (0, 0, 0)),
                  pl.BlockSpec((NCH, 1, 128), lambda b, i: (0, 0, 0)),
                  pl.BlockSpec((NCH, 1, 128), lambda b, i: (0, 0, 0)),
                  pl.BlockSpec((NCH, 1, 128), lambda b, i: (0, 0, 0))],
        out_specs=pl.BlockSpec((TM, C), lambda b, i: (b * nt + i, 0)),
        out_shape=jax.ShapeDtypeStruct((T, C), BF16),
        scratch_shapes=[pltpu.VMEM((NCH, HALO + TM, 128), F32),
                        pltpu.VMEM((NCH, TM, 128), F32)],
        compiler_params=_cparams(("parallel", "arbitrary")),
        name="conformer",
    )(u, w3, chunked(dw_b), chunked(ln_g), chunked(ln_b))


def _router_kernel(x_ref, g_ref, w2_ref, whi_ref, b_ref, h_ref, idx_ref, cw_ref, cnt_ref, *, TM, NG, EPG):
    i = pl.program_id(0)

    @pl.when(i == 0)
    def _():
        cnt_ref[...] = jnp.zeros_like(cnt_ref)

    x = x_ref[...]
    ms = jnp.mean(x * x, axis=-1, keepdims=True)
    h = x * lax.rsqrt(ms + RMS_EPS) * g_ref[...]
    h_ref[...] = h

    h_hi = h.astype(BF16)
    h_lo = (h - h_hi.astype(F32)).astype(BF16)
    p2 = jnp.dot(h_hi, w2_ref[...], preferred_element_type=F32)
    p1 = jnp.dot(h_lo, whi_ref[...], preferred_element_type=F32)
    L = ROUTER_LANES
    lg = p2[:, 0:L] + p2[:, L:2 * L] + p1 + b_ref[...]

    lane_i = lax.broadcasted_iota(jnp.int32, (TM, L), 1)
    lane = lane_i.astype(F32)
    gl = jnp.where(lane_i < NG, lg, NEG_BIG)
    gmax = jnp.max(gl, axis=-1, keepdims=True)
    gidx = jnp.min(jnp.where(gl == gmax, lane, float(L)), axis=-1, keepdims=True)
    p_top = 1.0 / jnp.sum(jnp.exp(gl - gmax), axis=-1, keepdims=True)

    egrp = jnp.floor((lane - EXPERT_LANE0) * (1.0 / EPG))
    sel = jnp.logical_and(lane_i >= EXPERT_LANE0, egrp == gidx)
    el = jnp.where(sel, lg, NEG_BIG)
    v0 = jnp.max(el, axis=-1, keepdims=True)
    i0 = jnp.min(jnp.where(el == v0, lane, float(L)), axis=-1, keepdims=True)
    el1 = jnp.where(lane == i0, NEG_BIG, el)
    v1 = jnp.max(el1, axis=-1, keepdims=True)
    i1 = jnp.min(jnp.where(el1 == v1, lane, float(L)), axis=-1, keepdims=True)
    t = jnp.exp(v1 - v0)
    w0 = 1.0 / (1.0 + t)
    c0 = p_top * w0
    c1 = p_top * (t * w0)

    oh0 = lane == i0
    oh1 = lane == i1
    oh = jnp.where(jnp.logical_or(oh0, oh1), 1.0, 0.0)
    r = lax.broadcasted_iota(jnp.int32, (TM, TM), 0)
    cidx = lax.broadcasted_iota(jnp.int32, (TM, TM), 1)
    strict = jnp.where(cidx < r, 1.0, 0.0).astype(BF16)
    before = jnp.dot(strict, oh.astype(BF16), preferred_element_type=F32) + cnt_ref[...]
    rank0 = jnp.sum(jnp.where(oh0, before, 0.0), axis=-1, keepdims=True)
    rank1 = jnp.sum(jnp.where(oh1, before, 0.0), axis=-1, keepdims=True)
    cnt_ref[...] = cnt_ref[...] + jnp.sum(oh, axis=0, keepdims=True)

    e0 = i0 - EXPERT_LANE0
    e1 = i1 - EXPERT_LANE0
    idx = jnp.where(lane_i == 0, e0, jnp.where(lane_i == 1, e1,
          jnp.where(lane_i == 2, rank0, jnp.where(lane_i == 3, rank1, 0.0))))
    idx_ref[...] = idx.astype(jnp.int32)
    cw_ref[...] = jnp.where(lane_i == 0, c0, jnp.where(lane_i == 1, c1, 0.0))


def _router(x1, g, r_grp, r_grp_b, r_exp, r_exp_b):
    T, D = x1.shape
    NG = r_grp.shape[1]
    EPG = r_exp.shape[2]
    NE = NG * EPG
    L = ROUTER_LANES
    assert NG <= EXPERT_LANE0 and EXPERT_LANE0 + NE <= L
    w = jnp.zeros((D, L), F32)
    w = w.at[:, 0:NG].set(r_grp).at[:, EXPERT_LANE0:EXPERT_LANE0 + NE].set(r_exp.reshape(D, NE))
    b = jnp.zeros((1, L), F32)
    b = b.at[0, 0:NG].set(r_grp_b).at[0, EXPERT_LANE0:EXPERT_LANE0 + NE].set(r_exp_b.reshape(NE))
    w_hi = w.astype(BF16)
    w_lo = (w - w_hi.astype(F32)).astype(BF16)
    w2 = jnp.concatenate([w_hi, w_lo], axis=1)
    TM = _tile(T, 256, 8)
    kern = functools.partial(_router_kernel, TM=TM, NG=NG, EPG=EPG)
    return pl.pallas_call(
        kern,
        grid=(T // TM,),
        in_specs=[pl.BlockSpec((TM, D), lambda i: (i, 0)),
                  pl.BlockSpec((1, D), lambda i: (0, 0)),
                  pl.BlockSpec((D, 2 * L), lambda i: (0, 0)),
                  pl.BlockSpec((D, L), lambda i: (0, 0)),
                  pl.BlockSpec((1, L), lambda i: (0, 0))],
        out_specs=[pl.BlockSpec((TM, D), lambda i: (i, 0)),
                   pl.BlockSpec((TM, L), lambda i: (i, 0)),
                   pl.BlockSpec((TM, L), lambda i: (i, 0)),
                   pl.BlockSpec((1, L), lambda i: (0, 0))],
        out_shape=[jax.ShapeDtypeStruct((T, D), F32),
                   jax.ShapeDtypeStruct((T, L), jnp.int32),
                   jax.ShapeDtypeStruct((T, L), F32),
                   jax.ShapeDtypeStruct((1, L), F32)],
        compiler_params=_cparams(("arbitrary",)),
        name="router",
    )(x1, g.reshape(1, D), w2, w_hi, b)


def _dispatch_kernel(pos0_ref, pos1_ref, h_ref, xs_in_ref, xs_ref, sem, *, TM):
    del xs_in_ref
    base = pl.program_id(0) * TM

    def copies(r):
        src = h_ref.at[pl.ds(r, 1)]
        return (pltpu.make_async_copy(src, xs_ref.at[pl.ds(pos0_ref[base + r], 1)], sem.at[0]),
                pltpu.make_async_copy(src, xs_ref.at[pl.ds(pos1_ref[base + r], 1)], sem.at[1]))

    def start(r, carry):
        for cp in copies(r):
            cp.start()
        return carry

    def wait(r, carry):
        for cp in copies(r):
            cp.wait()
        return carry

    lax.fori_loop(0, TM, start, 0)
    lax.fori_loop(0, TM, wait, 0)


def _dispatch(h2, pos0, pos1, rows):
    T, D = h2.shape
    TM = _tile(T, 256, 8)
    xs0 = jnp.zeros((rows, D), h2.dtype)
    grid_spec = pltpu.PrefetchScalarGridSpec(
        num_scalar_prefetch=2,
        grid=(T // TM,),
        in_specs=[pl.BlockSpec((TM, D), lambda i, p0, p1: (i, 0)),
                  pl.BlockSpec(memory_space=pl.ANY)],
        out_specs=pl.BlockSpec(memory_space=pl.ANY),
        scratch_shapes=[pltpu.SemaphoreType.DMA((2,))],
    )
    return pl.pallas_call(
        functools.partial(_dispatch_kernel, TM=TM),
        grid_spec=grid_spec,
        out_shape=jax.ShapeDtypeStruct((rows, D), h2.dtype),
        input_output_aliases={3: 0},
        compiler_params=_cparams(("arbitrary",)),
        name="moe_dispatch",
    )(pos0, pos1, h2, xs0)


def _expert_kernel(te_ref, ts_ref, nu_ref, x_ref, wg_ref, wu_ref, wd_ref, y_ref):
    del te_ref, ts_ref
    i = pl.program_id(0)
    j = pl.program_id(1)

    @pl.when(i < nu_ref[0])
    def _():
        x = x_ref[...].astype(BF16)
        g = jnp.dot(x, wg_ref[...].astype(BF16), preferred_element_type=F32)
        u = jnp.dot(x, wu_ref[...].astype(BF16), preferred_element_type=F32)
        hid = (_silu(g) * u).astype(BF16)
        part = jnp.dot(hid, wd_ref[...].astype(BF16), preferred_element_type=F32)

        @pl.when(j == 0)
        def _():
            y_ref[...] = part

        @pl.when(j != 0)
        def _():
            y_ref[...] = y_ref[...] + part

    @pl.when(jnp.logical_and(i >= nu_ref[0], j == 0))
    def _():
        y_ref[...] = jnp.zeros_like(y_ref)


def _experts(xs, w_gate, w_up, w_down, tile_expert, tile_src, n_used, tme):
    R, D = xs.shape
    NE, _, F = w_gate.shape
    fh = _tile(F, 256)
    nt = R // tme
    grid_spec = pltpu.PrefetchScalarGridSpec(
        num_scalar_prefetch=3,
        grid=(nt, F // fh),
        in_specs=[pl.BlockSpec((tme, D), lambda i, j, te, ts, nu: (ts[i], 0)),
                  pl.BlockSpec((None, D, fh), lambda i, j, te, ts, nu: (te[i], 0, j)),
                  pl.BlockSpec((None, D, fh), lambda i, j, te, ts, nu: (te[i], 0, j)),
                  pl.BlockSpec((None, fh, D), lambda i, j, te, ts, nu: (te[i], j, 0))],
        out_specs=pl.BlockSpec((tme, D), lambda i, j, te, ts, nu: (i, 0)),
    )
    return pl.pallas_call(
        _expert_kernel,
        grid_spec=grid_spec,
        out_shape=jax.ShapeDtypeStruct((R, D), F32),
        compiler_params=_cparams(("arbitrary", "arbitrary")),
        name="moe_experts",
    )(tile_expert, tile_src, n_used, xs, w_gate, w_up, w_down)


def _combine_kernel(pos0_ref, pos1_ref, x_ref, cw_ref, g_ref, y_ref, o_ref, y0_buf, y1_buf, sem, *, TM, final_norm):
    base = pl.program_id(0) * TM

    def copies(r):
        return (pltpu.make_async_copy(y_ref.at[pl.ds(pos0_ref[base + r], 1)], y0_buf.at[pl.ds(r, 1)], sem.at[0]),
                pltpu.make_async_copy(y_ref.at[pl.ds(pos1_ref[base + r], 1)], y1_buf.at[pl.ds(r, 1)], sem.at[1]))

    def start(r, carry):
        for cp in copies(r):
            cp.start()
        return carry

    def wait(r, carry):
        for cp in copies(r):
            cp.wait()
        return carry

    lax.fori_loop(0, TM, start, 0)
    lax.fori_loop(0, TM, wait, 0)
    cw = cw_ref[...]
    x = x_ref[...] + cw[:, 0:1] * y0_buf[...] + cw[:, 1:2] * y1_buf[...]
    if final_norm:
        ms = jnp.mean(x * x, axis=-1, keepdims=True)
        x = x * lax.rsqrt(ms + RMS_EPS) * g_ref[...]
    o_ref[...] = x


def _combine(x1, cw, y, pos0, pos1, g, final_norm):
    T, D = x1.shape
    TM = _tile(T, 128, 8)
    L = cw.shape[1]
    grid_spec = pltpu.PrefetchScalarGridSpec(
        num_scalar_prefetch=2,
        grid=(T // TM,),
        in_specs=[pl.BlockSpec((TM, D), lambda i, p0, p1: (i, 0)),
                  pl.BlockSpec((TM, L), lambda i, p0, p1: (i, 0)),
                  pl.BlockSpec((1, D), lambda i, p0, p1: (0, 0)),
                  pl.BlockSpec(memory_space=pl.ANY)],
        out_specs=pl.BlockSpec((TM, D), lambda i, p0, p1: (i, 0)),
        scratch_shapes=[pltpu.VMEM((TM, D), F32), pltpu.VMEM((TM, D), F32),
                        pltpu.SemaphoreType.DMA((2,))],
    )
    return pl.pallas_call(
        functools.partial(_combine_kernel, TM=TM, final_norm=final_norm),
        grid_spec=grid_spec,
        out_shape=jax.ShapeDtypeStruct((T, D), F32),
        compiler_params=_cparams(("arbitrary",)),
        name="moe_combine",
    )(pos0, pos1, x1, cw, g.reshape(1, D), y)


def _moe(x1, norm_g, r_grp, r_grp_b, r_exp, r_exp_b, w_gate, w_up, w_down, final_g, final_norm):
    T, D = x1.shape
    NG, EPG = r_exp.shape[1], r_exp.shape[2]
    NE = NG * EPG
    F = w_gate.shape[-1]
    tme = 256
    h2, idx, cw, cnt = _router(x1, norm_g, r_grp, r_grp_b, r_exp, r_exp_b)

    e0, e1, rank0, rank1 = idx[:, 0], idx[:, 1], idx[:, 2], idx[:, 3]
    counts = cnt[0, EXPERT_LANE0:EXPERT_LANE0 + NE].astype(jnp.int32)
    padded = ((counts + tme - 1) // tme) * tme
    ends = jnp.cumsum(padded)
    starts = ends - padded
    pos0 = starts[e0] + rank0
    pos1 = starts[e1] + rank1
    nt = (2 * T + NE * (tme - 1) + tme - 1) // tme
    n_used = (ends[-1] // tme).astype(jnp.int32)
    tile_ids = jnp.minimum(jnp.arange(nt, dtype=jnp.int32), n_used - 1)
    tile_expert = jnp.minimum(
        jnp.searchsorted(ends, tile_ids * tme, side="right").astype(jnp.int32), NE - 1)

    xs = _dispatch(h2, pos0, pos1, nt * tme)
    y = _experts(xs, w_gate.reshape(NE, D, F), w_up.reshape(NE, D, F), w_down.reshape(NE, F, D),
                 tile_expert, tile_ids, n_used.reshape(1), tme)
    return _combine(x1, cw, y, pos0, pos1, final_g, final_norm)


def kernel(x, norm_mix, w_in, ssd_conv_w, ssd_conv_b, ssd_dt_bias, ssd_a_log, ssd_d, ssd_norm, ssd_w_out, conf_glu_b, conf_dw_w, conf_dw_b, conf_ln_g, conf_ln_b, conf_w_out, w_out, norm_ffn, router_group, router_group_b, router_expert, router_expert_b, expert_w_gate, expert_w_up, expert_w_down, norm_final):
    B, S, D = x.shape
    T = B * S
    depth = w_in.shape[0]
    d_ssd = ssd_norm.shape[1]
    d_xbc = ssd_conv_w.shape[2]
    H = ssd_a_log.shape[1]
    d_conf = conf_dw_w.shape[2]
    o_xbc = d_ssd
    o_dt = o_xbc + d_xbc
    o_glu = o_dt + H
    o_gate = o_glu + 2 * d_conf

    xf = x.reshape(T, D)
    for l in range(depth):
        wl = w_in[l]
        w_z = wl[:, 0:o_xbc].astype(BF16)
        w_xbc = wl[:, o_xbc:o_dt].astype(BF16)
        w_dt = wl[:, o_dt:o_glu].astype(BF16)
        w_glu_u = wl[:, o_glu:o_glu + d_conf].astype(BF16)
        w_glu_g = wl[:, o_glu + d_conf:o_gate].astype(BF16)
        w_gates = wl[:, o_gate:].astype(BF16)

        h = _rmsnorm(xf, norm_mix[l], BF16)
        zs = _proj_act(h, w_z, "silu", BF16, "proj_z")
        xbc = _proj_act(h, w_xbc, "none", BF16, "proj_xbc")
        dt, dtT = _proj_dt(h, w_dt, ssd_dt_bias[l])
        glu = _proj_glu(h, w_glu_u, w_glu_g, conf_glu_b[l, :d_conf], conf_glu_b[l, d_conf:], BF16)
        gates = _proj_act(h, w_gates, "sigmoid", BF16, "proj_gates")

        yn = _ssd(xbc, zs, dt, dtT, ssd_conv_w[l], ssd_conv_b[l], ssd_a_log[l], ssd_d[l], ssd_norm[l],
                  batch=B, seq=S, d_ssd=d_ssd)
        uc = _conformer(glu, conf_dw_w[l], conf_dw_b[l], conf_ln_g[l], conf_ln_b[l], batch=B, seq=S)

        m1 = _out_gate(yn, ssd_w_out[l].astype(BF16), gates, 0, None, F32, "out_ssd")
        mixed = _out_gate(uc, conf_w_out[l].astype(BF16), gates, D, m1, BF16, "out_conf")
        x1 = _out_res(mixed, w_out[l].astype(BF16), xf)

        xf = _moe(x1, norm_ffn[l], router_group[l], router_group_b[l], router_expert[l],
                  router_expert_b[l], expert_w_gate[l], expert_w_up[l], expert_w_down[l],
                  norm_final, l == depth - 1)
    return xf.reshape(B, S, D)
```

```python
import functools

import jax
import jax.numpy as jnp
from jax import lax
from jax.experimental import pallas as pl
from jax.experimental.pallas import tpu as pltpu

SSD_GROUPS = 8
SSD_CHUNK = 128
RMS_EPS = 1e-6
LN_EPS = 1e-5
NEG_BIG = -1e30
LANES = 128
DMA_UNROLL = 8
ROUTER_LANES = 128
EXPERT_LANE0 = 64
VMEM_LIMIT_BYTES = 56 * 1024 * 1024

F32 = jnp.float32
BF16 = jnp.bfloat16


def _cparams(semantics):
    return pltpu.CompilerParams(dimension_semantics=semantics,
                                vmem_limit_bytes=VMEM_LIMIT_BYTES)


def _tile(n, pref, mult=128):
    if n <= pref:
        return n
    t = (pref // mult) * mult
    while t >= mult:
        if n % t == 0:
            return t
        t -= mult
    return n


def _sigmoid(x):
    return 0.5 + 0.5 * jnp.tanh(0.5 * x)


def _silu(x):
    h = 0.5 * x
    return h + h * jnp.tanh(h)


def _softplus(x):
    return jnp.maximum(x, 0.0) + jnp.log(1.0 + jnp.exp(-jnp.abs(x)))


def _rmsnorm_kernel(x_ref, g_ref, o_ref):
    x = x_ref[...]
    ms = jnp.mean(x * x, axis=-1, keepdims=True)
    o_ref[...] = (x * lax.rsqrt(ms + RMS_EPS) * g_ref[...]).astype(o_ref.dtype)


def _rmsnorm(x, g, out_dtype):
    T, D = x.shape
    tm = _tile(T, 256, 8)
    return pl.pallas_call(
        _rmsnorm_kernel,
        grid=(T // tm,),
        in_specs=[pl.BlockSpec((tm, D), lambda i: (i, 0)),
                  pl.BlockSpec((1, D), lambda i: (0, 0))],
        out_specs=pl.BlockSpec((tm, D), lambda i: (i, 0)),
        out_shape=jax.ShapeDtypeStruct((T, D), out_dtype),
        compiler_params=_cparams(("parallel",)),
        name="rmsnorm",
    )(x, g.reshape(1, D))


def _wcols(K, tn, col0):
    assert col0 % LANES == 0 and tn % LANES == 0
    return pl.BlockSpec((pl.Element(K), pl.Element(tn)),
                        lambda j, i: (0, (col0 // LANES + j * (tn // LANES)) * LANES))


def _proj_act_kernel(a_ref, w_ref, o_ref, w16_ref, *, act):
    @pl.when(pl.program_id(1) == 0)
    def _():
        w16_ref[...] = w_ref[...].astype(BF16)

    acc = jnp.dot(a_ref[...], w16_ref[...], preferred_element_type=F32)
    if act == "silu":
        acc = _silu(acc)
    elif act == "sigmoid":
        acc = _sigmoid(acc)
    o_ref[...] = acc.astype(o_ref.dtype)


def _proj_act(a, w, col0, n, act, out_dtype, name):
    M, K = a.shape
    tm, tn = _tile(M, 1024, 8), _tile(n, 512)
    return pl.pallas_call(
        functools.partial(_proj_act_kernel, act=act),
        grid=(n // tn, M // tm),
        in_specs=[pl.BlockSpec((tm, K), lambda j, i: (i, 0)),
                  _wcols(K, tn, col0)],
        out_specs=pl.BlockSpec((tm, tn), lambda j, i: (i, j)),
        out_shape=jax.ShapeDtypeStruct((M, n), out_dtype),
        scratch_shapes=[pltpu.VMEM((K, tn), BF16)],
        compiler_params=_cparams(("parallel", "arbitrary")),
        name=name,
    )(a, w)


def _proj_glu_kernel(a_ref, wu_ref, wg_ref, bu_ref, bg_ref, o_ref, wu16_ref, wg16_ref):
    @pl.when(pl.program_id(1) == 0)
    def _():
        wu16_ref[...] = wu_ref[...].astype(BF16)
        wg16_ref[...] = wg_ref[...].astype(BF16)

    a = a_ref[...]
    u = jnp.dot(a, wu16_ref[...], preferred_element_type=F32) + bu_ref[...]
    g = jnp.dot(a, wg16_ref[...], preferred_element_type=F32) + bg_ref[...]
    o_ref[...] = (u * _sigmoid(g)).astype(o_ref.dtype)


def _proj_glu(a, w, col_u, col_g, n, bu, bg, out_dtype):
    M, K = a.shape
    tm, tn = _tile(M, 1024, 8), _tile(n, 256)
    return pl.pallas_call(
        _proj_glu_kernel,
        grid=(n // tn, M // tm),
        in_specs=[pl.BlockSpec((tm, K), lambda j, i: (i, 0)),
                  _wcols(K, tn, col_u),
                  _wcols(K, tn, col_g),
                  pl.BlockSpec((1, tn), lambda j, i: (0, j)),
                  pl.BlockSpec((1, tn), lambda j, i: (0, j))],
        out_specs=pl.BlockSpec((tm, tn), lambda j, i: (i, j)),
        out_shape=jax.ShapeDtypeStruct((M, n), out_dtype),
        scratch_shapes=[pltpu.VMEM((K, tn), BF16), pltpu.VMEM((K, tn), BF16)],
        compiler_params=_cparams(("parallel", "arbitrary")),
        name="proj_glu",
    )(a, w, w, bu.reshape(1, n), bg.reshape(1, n))


def _proj_dt_kernel(a_ref, w_ref, b_ref, dt_ref, dtT_ref):
    acc = jnp.dot(a_ref[...], w_ref[...].astype(BF16), preferred_element_type=F32) + b_ref[...]
    dt = _softplus(acc)
    dt_ref[...] = dt
    dtT_ref[...] = dt.T


def _proj_dt(a, w, col0, H, b):
    M, K = a.shape
    tm = _tile(M, 512, 128)
    return pl.pallas_call(
        _proj_dt_kernel,
        grid=(M // tm,),
        in_specs=[pl.BlockSpec((tm, K), lambda i: (i, 0)),
                  pl.BlockSpec((pl.Element(K), pl.Element(H)), lambda i: (0, col0)),
                  pl.BlockSpec((1, H), lambda i: (0, 0))],
        out_specs=[pl.BlockSpec((tm, H), lambda i: (i, 0)),
                   pl.BlockSpec((H, tm), lambda i: (0, i))],
        out_shape=[jax.ShapeDtypeStruct((M, H), F32),
                   jax.ShapeDtypeStruct((H, M), F32)],
        compiler_params=_cparams(("parallel",)),
        name="proj_dt",
    )(a, w, b.reshape(1, H))


def _out_gate_kernel(*refs, has_prev):
    if has_prev:
        a_ref, w_ref, g_ref, p_ref, o_ref = refs
    else:
        a_ref, w_ref, g_ref, o_ref = refs
    acc = jnp.dot(a_ref[...], w_ref[...], preferred_element_type=F32)
    acc = acc * g_ref[...].astype(F32)
    if has_prev:
        acc = acc + p_ref[...].astype(F32)
    o_ref[...] = acc.astype(o_ref.dtype)


def _out_gate(a, w, gates, gate_col0, prev, out_dtype, name):
    M, K = a.shape
    N = w.shape[1]
    tm, tn = _tile(M, 512, 8), _tile(N, 512)
    goff = gate_col0 // tn
    in_specs = [pl.BlockSpec((tm, K), lambda j, i: (i, 0)),
                pl.BlockSpec((K, tn), lambda j, i: (0, j)),
                pl.BlockSpec((tm, tn), lambda j, i: (i, j + goff))]
    args = [a, w, gates]
    if prev is not None:
        in_specs.append(pl.BlockSpec((tm, tn), lambda j, i: (i, j)))
        args.append(prev)
    return pl.pallas_call(
        functools.partial(_out_gate_kernel, has_prev=prev is not None),
        grid=(N // tn, M // tm),
        in_specs=in_specs,
        out_specs=pl.BlockSpec((tm, tn), lambda j, i: (i, j)),
        out_shape=jax.ShapeDtypeStruct((M, N), out_dtype),
        compiler_params=_cparams(("parallel", "parallel")),
        name=name,
    )(*args)


def _out_res_kernel(a_ref, w_ref, r_ref, o_ref):
    acc = jnp.dot(a_ref[...], w_ref[...], preferred_element_type=F32)
    o_ref[...] = r_ref[...] + acc


def _out_res(a, w, res):
    M, K = a.shape
    N = w.shape[1]
    tm, tn = _tile(M, 512, 8), _tile(N, 1024)
    return pl.pallas_call(
        _out_res_kernel,
        grid=(N // tn, M // tm),
        in_specs=[pl.BlockSpec((tm, K), lambda j, i: (i, 0)),
                  pl.BlockSpec((K, tn), lambda j, i: (0, j)),
                  pl.BlockSpec((tm, tn), lambda j, i: (i, j))],
        out_specs=pl.BlockSpec((tm, tn), lambda j, i: (i, j)),
        out_shape=jax.ShapeDtypeStruct((M, N), F32),
        compiler_params=_cparams(("parallel", "parallel")),
        name="out_res",
    )(a, w, res)


def _split3(v):
    hi = v.astype(BF16)
    r1 = v - hi.astype(F32)
    mid = r1.astype(BF16)
    lo = (r1 - mid.astype(F32)).astype(BF16)
    return hi, mid, lo


def _ssd_kernel(xs_ref, b_ref, c_ref, wx_ref, wb_ref, wc_ref, bx_ref, bb_ref, bc_ref,
                dt_ref, dtT_ref, alog_ref, alogT_ref, z_ref, dskip_ref, ng_ref,
                o_ref, ext_ref, state_ref, y_ref, *, Q, Hg, P, N, K):
    W = Hg * P
    HALO = 8
    NPL = Hg // 2
    c = pl.program_id(2)

    @pl.when(c == 0)
    def _():
        ext_ref[:, 0:HALO, :] = jnp.zeros((NPL + 2, HALO, 2 * P), F32)
        state_ref[...] = jnp.zeros_like(state_ref)

    for p in range(NPL):
        ext_ref[p, HALO:HALO + Q, :] = xs_ref[:, p * 2 * P:(p + 1) * 2 * P].astype(F32)
    ext_ref[NPL, HALO:HALO + Q, :] = b_ref[...].astype(F32)
    ext_ref[NPL + 1, HALO:HALO + Q, :] = c_ref[...].astype(F32)

    def conv_silu(plane, w_ref, bias_ref, wlo):
        acc = bias_ref[:, wlo:wlo + 2 * P]
        for k in range(K):
            r0 = HALO - (K - 1) + k
            acc = acc + w_ref[k:k + 1, wlo:wlo + 2 * P] * ext_ref[plane, r0:r0 + Q, :]
        return _silu(acc)

    bm = conv_silu(NPL, wb_ref, bb_ref, 0)
    cm = conv_silu(NPL + 1, wc_ref, bc_ref, 0)

    dt = dt_ref[...]
    dtT = dtT_ref[...]
    a = dt * (-jnp.exp(alog_ref[...]))
    aT = dtT * (-jnp.exp(alogT_ref[...]))

    row = lax.broadcasted_iota(jnp.int32, (Q, Q), 0)
    col = lax.broadcasted_iota(jnp.int32, (Q, Q), 1)
    causal = col <= row
    tril = jnp.where(causal, 1.0, 0.0).astype(BF16)
    triu = jnp.where(row <= col, 1.0, 0.0).astype(BF16)
    acum = jnp.dot(jnp.concatenate([tril, tril, tril], axis=1),
                   jnp.concatenate(_split3(a), axis=0), preferred_element_type=F32)
    acumT = jnp.dot(jnp.concatenate(_split3(aT), axis=1),
                    jnp.concatenate([triu, triu, triu], axis=0), preferred_element_type=F32)

    bm16 = bm.astype(BF16)
    cm16 = cm.astype(BF16)
    cb = lax.dot_general(cm16, bm16, (((1,), (1,)), ((), ())), preferred_element_type=F32)
    bT = bm.T

    lane = lax.broadcasted_iota(jnp.int32, (1, 2 * P), 1)
    lo_half = lane < P
    zero16 = jnp.zeros((), BF16)

    ssq = jnp.zeros((Q, 1), F32)
    for q in range(Hg // 2):
        l0 = q * 2 * P
        x_pair = conv_silu(q, wx_ref, bx_ref, l0)
        x16 = x_pair.astype(BF16)
        rhs_x = jnp.concatenate([jnp.where(lo_half, x16, zero16),
                                 jnp.where(lo_half, zero16, x16)], axis=0)
        s_old = state_ref[q]
        s16 = s_old.astype(BF16)
        rhs_s = jnp.concatenate([jnp.where(lo_half, s16, zero16),
                                 jnp.where(lo_half, zero16, s16)], axis=0)
        l_parts, ec_parts, bw_parts, g_parts = [], [], [], []
        for hh in range(2):
            h = 2 * q + hh
            ai = jnp.broadcast_to(acum[:, h:h + 1], (Q, Q))
            aj = acumT[h:h + 1, :]
            dtj = dtT[h:h + 1, :]
            seg = jnp.where(causal, ai - aj, NEG_BIG)
            l_parts.append((jnp.exp(seg) * cb * dtj).astype(BF16))
            ec_parts.append((jnp.exp(ai) * cm).astype(BF16))
            a_last = acumT[h:h + 1, Q - 1:Q]
            w_end = jnp.exp(a_last - aj) * dtj
            bw_parts.append((bT * w_end).astype(BF16))
            g_parts.append(jnp.exp(a_last))
        lhs_y = jnp.concatenate(l_parts + ec_parts, axis=1)
        rhs_y = jnp.concatenate([rhs_x, rhs_s], axis=0)
        y = jnp.dot(lhs_y, rhs_y, preferred_element_type=F32)
        upd = jnp.dot(jnp.concatenate(bw_parts, axis=1), rhs_x,
                      preferred_element_type=F32)
        decay = jnp.where(lo_half, g_parts[0], g_parts[1])
        state_ref[q] = s_old * decay + upd

        y = y + dskip_ref[:, l0:l0 + 2 * P] * x_pair
        y = y * z_ref[:, l0:l0 + 2 * P].astype(F32)
        y_ref[:, l0:l0 + 2 * P] = y
        ssq = ssq + jnp.sum(y * y, axis=-1, keepdims=True)

    ext_ref[:, 0:HALO, :] = ext_ref[:, Q:Q + HALO, :]
    rstd = lax.rsqrt(ssq * (1.0 / W) + RMS_EPS)
    o_ref[...] = (y_ref[...] * rstd * ng_ref[...]).astype(o_ref.dtype)


def _ssd(xbc, zs, dt, dtT, conv_w, conv_b, a_log, d_skip, norm_g, *, batch, seq, d_ssd):
    T = batch * seq
    G = SSD_GROUPS
    Q = SSD_CHUNK
    H = a_log.shape[0]
    Hg = H // G
    P = d_ssd // H
    W = Hg * P
    N = (xbc.shape[1] - d_ssd) // (2 * G)
    K = conv_w.shape[0]
    assert 2 * P == 128 and N == 128 and Q == 128 and Hg % 2 == 0 and W % 128 == 0
    assert seq % Q == 0 and K - 1 <= 8
    nc = seq // Q
    nb_x = d_ssd // N

    dt_g = dt.reshape(T, G, Hg).transpose(1, 0, 2)
    alog_g = a_log.reshape(G, 1, Hg)
    alogT = a_log.reshape(H, 1)
    dskip = jnp.repeat(d_skip, P).reshape(1, d_ssd)
    cb2 = conv_b.reshape(1, -1)

    rowblk = lambda b, g, c: b * nc + c
    kern = functools.partial(_ssd_kernel, Q=Q, Hg=Hg, P=P, N=N, K=K)
    return pl.pallas_call(
        kern,
        grid=(batch, G, nc),
        in_specs=[
            pl.BlockSpec((Q, W), lambda b, g, c: (rowblk(b, g, c), g)),
            pl.BlockSpec((Q, N), lambda b, g, c: (rowblk(b, g, c), nb_x + g)),
            pl.BlockSpec((Q, N), lambda b, g, c: (rowblk(b, g, c), nb_x + G + g)),
            pl.BlockSpec((K, W), lambda b, g, c: (0, g)),
            pl.BlockSpec((K, N), lambda b, g, c: (0, nb_x + g)),
            pl.BlockSpec((K, N), lambda b, g, c: (0, nb_x + G + g)),
            pl.BlockSpec((1, W), lambda b, g, c: (0, g)),
            pl.BlockSpec((1, N), lambda b, g, c: (0, nb_x + g)),
            pl.BlockSpec((1, N), lambda b, g, c: (0, nb_x + G + g)),
            pl.BlockSpec((None, Q, Hg), lambda b, g, c: (g, rowblk(b, g, c), 0)),
            pl.BlockSpec((Hg, Q), lambda b, g, c: (g, rowblk(b, g, c))),
            pl.BlockSpec((None, 1, Hg), lambda b, g, c: (g, 0, 0)),
            pl.BlockSpec((Hg, 1), lambda b, g, c: (g, 0)),
            pl.BlockSpec((Q, W), lambda b, g, c: (rowblk(b, g, c), g)),
            pl.BlockSpec((1, W), lambda b, g, c: (0, g)),
            pl.BlockSpec((1, W), lambda b, g, c: (0, g)),
        ],
        out_specs=pl.BlockSpec((Q, W), lambda b, g, c: (rowblk(b, g, c), g)),
        out_shape=jax.ShapeDtypeStruct((T, d_ssd), BF16),
        scratch_shapes=[pltpu.VMEM((Hg // 2 + 2, Q + 8, 2 * P), F32),
                        pltpu.VMEM((Hg // 2, N, 2 * P), F32),
                        pltpu.VMEM((Q, W), F32)],
        compiler_params=_cparams(("parallel", "parallel", "arbitrary")),
        name="ssd_scan",
    )(xbc, xbc, xbc, conv_w, conv_w, conv_w, cb2, cb2, cb2,
      dt_g, dtT, alog_g, alogT, zs, dskip, norm_g.reshape(1, d_ssd))


def _conf_kernel(u_ref, w_ref, b_ref, g_ref, beta_ref, o_ref, ext_ref, conv_ref, *, TM, C, K, HALO):
    NCH = C // 128
    i = pl.program_id(1)

    @pl.when(i == 0)
    def _():
        ext_ref[:, 0:HALO, :] = jnp.zeros((NCH, HALO, 128), F32)

    for ch in range(NCH):
        ext_ref[ch, HALO:HALO + TM, :] = u_ref[:, ch * 128:(ch + 1) * 128].astype(F32)

    def chunk_body(ch, carry):
        s1, = carry
        acc = jnp.broadcast_to(b_ref[ch], (TM, 128))
        for k in range(K):
            r0 = HALO - (K - 1) + k
            acc = acc + w_ref[ch, k:k + 1, :] * ext_ref[ch, r0:r0 + TM, :]
        conv_ref[ch] = acc
        ext_ref[ch, 0:HALO, :] = ext_ref[ch, TM:TM + HALO, :]
        return (s1 + acc,)

    s1, = lax.fori_loop(0, NCH, chunk_body, (jnp.zeros((TM, 128), F32),))
    mu = jnp.sum(s1, axis=-1, keepdims=True) * (1.0 / C)

    def var_body(ch, s2):
        d = conv_ref[ch] - mu
        return s2 + d * d

    s2 = lax.fori_loop(0, NCH, var_body, jnp.zeros((TM, 128), F32))
    rstd = lax.rsqrt(jnp.sum(s2, axis=-1, keepdims=True) * (1.0 / C) + LN_EPS)
    for ch in range(NCH):
        v = (conv_ref[ch] - mu) * rstd * g_ref[ch] + beta_ref[ch]
        o_ref[:, ch * 128:(ch + 1) * 128] = _silu(v).astype(o_ref.dtype)


def _conformer(u, dw_w, dw_b, ln_g, ln_b, *, batch, seq):
    T, C = u.shape
    K = dw_w.shape[0]
    HALO = 32
    assert K - 1 <= HALO and C % 128 == 0
    TM = _tile(seq, 128, 32)
    nt = seq // TM
    NCH = C // 128
    w3 = dw_w.reshape(K, NCH, 128).transpose(1, 0, 2)
    chunked = lambda v: v.reshape(NCH, 1, 128)
    kern = functools.partial(_conf_kernel, TM=TM, C=C, K=K, HALO=HALO)
    return pl.pallas_call(
        kern,
        grid=(batch, nt),
        in_specs=[pl.BlockSpec((TM, C), lambda b, i: (b * nt + i, 0)),
                  pl.BlockSpec((NCH, K, 128), lambda b, i: (0, 0, 0)),
                  pl.BlockSpec((NCH, 1, 128), lambda b, i: (0, 0, 0)),
                  pl.BlockSpec((NCH, 1, 128), lambda b, i: (0, 0, 0)),
                  pl.BlockSpec((NCH, 1, 128), lambda b, i: (0, 0, 0))],
        out_specs=pl.BlockSpec((TM, C), lambda b, i: (b * nt + i, 0)),
        out_shape=jax.ShapeDtypeStruct((T, C), BF16),
        scratch_shapes=[pltpu.VMEM((NCH, HALO + TM, 128), F32),
                        pltpu.VMEM((NCH, TM, 128), F32)],
        compiler_params=_cparams(("parallel", "arbitrary")),
        name="conformer",
    )(u, w3, chunked(dw_b), chunked(ln_g), chunked(ln_b))


def _pack_halves(v):
    d2 = v.shape[1] // 2
    hi = lax.bitcast_convert_type(v[:, :d2].astype(BF16).astype(F32), jnp.uint32)
    lo = lax.bitcast_convert_type(v[:, d2:].astype(BF16).astype(F32), jnp.uint32)
    return hi | (lo >> 16)


def _unpack_halves(p):
    hi = lax.bitcast_convert_type(p & jnp.uint32(0xFFFF0000), F32)
    lo = lax.bitcast_convert_type(p << 16, F32)
    return hi, lo


def _router_kernel(x_ref, g_ref, w2_ref, whi_ref, b_ref, h_ref, idx_ref, cw_ref, cnt_ref, *, TM, NG, EPG):
    i = pl.program_id(0)

    @pl.when(i == 0)
    def _():
        cnt_ref[...] = jnp.zeros_like(cnt_ref)

    x = x_ref[...]
    ms = jnp.mean(x * x, axis=-1, keepdims=True)
    h = x * lax.rsqrt(ms + RMS_EPS) * g_ref[...]
    h_ref[...] = _pack_halves(h)

    h_hi = h.astype(BF16)
    h_lo = (h - h_hi.astype(F32)).astype(BF16)
    p2 = jnp.dot(h_hi, w2_ref[...], preferred_element_type=F32)
    p1 = jnp.dot(h_lo, whi_ref[...], preferred_element_type=F32)
    L = ROUTER_LANES
    lg = p2[:, 0:L] + p2[:, L:2 * L] + p1 + b_ref[...]

    lane_i = lax.broadcasted_iota(jnp.int32, (TM, L), 1)
    lane = lane_i.astype(F32)
    gl = jnp.where(lane_i < NG, lg, NEG_BIG)
    gmax = jnp.max(gl, axis=-1, keepdims=True)
    gidx = jnp.min(jnp.where(gl == gmax, lane, float(L)), axis=-1, keepdims=True)
    p_top = 1.0 / jnp.sum(jnp.exp(gl - gmax), axis=-1, keepdims=True)

    egrp = jnp.floor((lane - EXPERT_LANE0) * (1.0 / EPG))
    sel = jnp.logical_and(lane_i >= EXPERT_LANE0, egrp == gidx)
    el = jnp.where(sel, lg, NEG_BIG)
    v0 = jnp.max(el, axis=-1, keepdims=True)
    i0 = jnp.min(jnp.where(el == v0, lane, float(L)), axis=-1, keepdims=True)
    el1 = jnp.where(lane == i0, NEG_BIG, el)
    v1 = jnp.max(el1, axis=-1, keepdims=True)
    i1 = jnp.min(jnp.where(el1 == v1, lane, float(L)), axis=-1, keepdims=True)
    t = jnp.exp(v1 - v0)
    w0 = 1.0 / (1.0 + t)
    c0 = p_top * w0
    c1 = p_top * (t * w0)

    oh0 = lane == i0
    oh1 = lane == i1
    oh = jnp.where(jnp.logical_or(oh0, oh1), 1.0, 0.0)
    r = lax.broadcasted_iota(jnp.int32, (TM, TM), 0)
    cidx = lax.broadcasted_iota(jnp.int32, (TM, TM), 1)
    strict = jnp.where(cidx < r, 1.0, 0.0).astype(BF16)
    before = jnp.dot(strict, oh.astype(BF16), preferred_element_type=F32) + cnt_ref[...]
    rank0 = jnp.sum(jnp.where(oh0, before, 0.0), axis=-1, keepdims=True)
    rank1 = jnp.sum(jnp.where(oh1, before, 0.0), axis=-1, keepdims=True)
    cnt_ref[...] = cnt_ref[...] + jnp.sum(oh, axis=0, keepdims=True)

    e0 = i0 - EXPERT_LANE0
    e1 = i1 - EXPERT_LANE0
    idx = jnp.where(lane_i == 0, e0, jnp.where(lane_i == 1, e1,
          jnp.where(lane_i == 2, rank0, jnp.where(lane_i == 3, rank1, 0.0))))
    idx_ref[...] = idx.astype(jnp.int32)
    cw_ref[...] = jnp.where(lane_i == 0, c0, jnp.where(lane_i == 1, c1, 0.0))


def _router(x1, g, r_grp, r_grp_b, r_exp, r_exp_b):
    T, D = x1.shape
    NG = r_grp.shape[1]
    EPG = r_exp.shape[2]
    NE = NG * EPG
    L = ROUTER_LANES
    assert NG <= EXPERT_LANE0 and EXPERT_LANE0 + NE <= L
    w = jnp.zeros((D, L), F32)
    w = w.at[:, 0:NG].set(r_grp).at[:, EXPERT_LANE0:EXPERT_LANE0 + NE].set(r_exp.reshape(D, NE))
    b = jnp.zeros((1, L), F32)
    b = b.at[0, 0:NG].set(r_grp_b).at[0, EXPERT_LANE0:EXPERT_LANE0 + NE].set(r_exp_b.reshape(NE))
    w_hi = w.astype(BF16)
    w_lo = (w - w_hi.astype(F32)).astype(BF16)
    w2 = jnp.concatenate([w_hi, w_lo], axis=1)
    TM = _tile(T, 256, 8)
    kern = functools.partial(_router_kernel, TM=TM, NG=NG, EPG=EPG)
    return pl.pallas_call(
        kern,
        grid=(T // TM,),
        in_specs=[pl.BlockSpec((TM, D), lambda i: (i, 0)),
                  pl.BlockSpec((1, D), lambda i: (0, 0)),
                  pl.BlockSpec((D, 2 * L), lambda i: (0, 0)),
                  pl.BlockSpec((D, L), lambda i: (0, 0)),
                  pl.BlockSpec((1, L), lambda i: (0, 0))],
        out_specs=[pl.BlockSpec((TM, D // 2), lambda i: (i, 0)),
                   pl.BlockSpec((TM, L), lambda i: (i, 0)),
                   pl.BlockSpec((TM, L), lambda i: (i, 0)),
                   pl.BlockSpec((1, L), lambda i: (0, 0))],
        out_shape=[jax.ShapeDtypeStruct((T, D // 2), jnp.uint32),
                   jax.ShapeDtypeStruct((T, L), jnp.int32),
                   jax.ShapeDtypeStruct((T, L), F32),
                   jax.ShapeDtypeStruct((1, L), F32)],
        compiler_params=_cparams(("arbitrary",)),
        name="router",
    )(x1, g.reshape(1, D), w2, w_hi, b)


def _positions_kernel(idx_ref, starts_ref, pos_ref):
    idx = idx_ref[...]
    lane = lax.broadcasted_iota(jnp.int32, idx.shape, 1)
    starts = starts_ref[...]

    def lookup(e):
        return jnp.sum(jnp.where(lane == e + EXPERT_LANE0, starts, 0.0), axis=-1, keepdims=True)

    pos0 = lookup(idx[:, 0:1]).astype(jnp.int32) + idx[:, 2:3]
    pos1 = lookup(idx[:, 1:2]).astype(jnp.int32) + idx[:, 3:4]
    pos_ref[...] = jnp.where(lane == 0, pos0, jnp.where(lane == 1, pos1, 0))


def _positions(idx, starts_row):
    T, L = idx.shape
    TM = _tile(T, 1024, 8)
    return pl.pallas_call(
        _positions_kernel,
        grid=(T // TM,),
        in_specs=[pl.BlockSpec((TM, L), lambda i: (i, 0)),
                  pl.BlockSpec((1, L), lambda i: (0, 0))],
        out_specs=pl.BlockSpec((TM, L), lambda i: (i, 0)),
        out_shape=jax.ShapeDtypeStruct((T, L), jnp.int32),
        compiler_params=_cparams(("parallel",)),
        name="moe_positions",
    )(idx, starts_row)


def _dispatch_kernel(pos0_ref, pos1_ref, h_ref, xs_in_ref, xs_ref, sem, *, TM):
    del xs_in_ref
    base = pl.program_id(0) * TM

    def copies(r):
        src = h_ref.at[pl.ds(r, 1)]
        return (pltpu.make_async_copy(src, xs_ref.at[pl.ds(pos0_ref[base + r], 1)], sem.at[0]),
                pltpu.make_async_copy(src, xs_ref.at[pl.ds(pos1_ref[base + r], 1)], sem.at[1]))

    def start(r, carry):
        for cp in copies(r):
            cp.start()
        return carry

    def wait(r, carry):
        for cp in copies(r):
            cp.wait()
        return carry

    lax.fori_loop(0, TM, start, 0, unroll=DMA_UNROLL)
    lax.fori_loop(0, TM, wait, 0, unroll=DMA_UNROLL)


def _dispatch(h2, pos0, pos1, rows):
    T, D2 = h2.shape
    TM = _tile(T, 256, 8)
    grid_spec = pltpu.PrefetchScalarGridSpec(
        num_scalar_prefetch=2,
        grid=(T // TM,),
        in_specs=[pl.BlockSpec((TM, D2), lambda i, p0, p1: (i, 0)),
                  pl.BlockSpec(memory_space=pl.ANY)],
        out_specs=pl.BlockSpec(memory_space=pl.ANY),
        scratch_shapes=[pltpu.SemaphoreType.DMA((2,))],
    )
    return pl.pallas_call(
        functools.partial(_dispatch_kernel, TM=TM),
        grid_spec=grid_spec,
        out_shape=jax.ShapeDtypeStruct((rows, D2), h2.dtype),
        input_output_aliases={3: 0},
        compiler_params=_cparams(("arbitrary",)),
        name="moe_dispatch",
    )(pos0, pos1, h2, jnp.zeros((rows, D2), h2.dtype))


def _expert_changed(te_ref, i):
    return jnp.logical_or(i == 0, te_ref[i] != te_ref[jnp.maximum(i - 1, 0)])


def _experts_up_kernel(te_ref, ts_ref, nu_ref, x_ref, wg_ref, wu_ref, hid_ref, wg16_ref, wu16_ref):
    del ts_ref
    i = pl.program_id(0)

    @pl.when(_expert_changed(te_ref, i))
    def _():
        wg16_ref[...] = wg_ref[...].astype(BF16)
        wu16_ref[...] = wu_ref[...].astype(BF16)

    @pl.when(i < nu_ref[0])
    def _():
        hi, lo = _unpack_halves(x_ref[...])
        x = jnp.concatenate([hi.astype(BF16), lo.astype(BF16)], axis=1)
        g = jnp.dot(x, wg16_ref[...], preferred_element_type=F32)
        u = jnp.dot(x, wu16_ref[...], preferred_element_type=F32)
        hid_ref[...] = (_silu(g) * u).astype(hid_ref.dtype)

    @pl.when(i >= nu_ref[0])
    def _():
        hid_ref[...] = jnp.zeros_like(hid_ref)


def _experts_up(xs, w_gate, w_up, tile_expert, tile_src, n_used, tme):
    R, D2 = xs.shape
    NE, D, F = w_gate.shape
    nt = R // tme
    grid_spec = pltpu.PrefetchScalarGridSpec(
        num_scalar_prefetch=3,
        grid=(nt,),
        in_specs=[pl.BlockSpec((tme, D2), lambda i, te, ts, nu: (ts[i], 0)),
                  pl.BlockSpec((None, D, F), lambda i, te, ts, nu: (te[i], 0, 0)),
                  pl.BlockSpec((None, D, F), lambda i, te, ts, nu: (te[i], 0, 0))],
        out_specs=pl.BlockSpec((tme, F), lambda i, te, ts, nu: (i, 0)),
        scratch_shapes=[pltpu.VMEM((D, F), BF16), pltpu.VMEM((D, F), BF16)],
    )
    return pl.pallas_call(
        _experts_up_kernel,
        grid_spec=grid_spec,
        out_shape=jax.ShapeDtypeStruct((R, F), BF16),
        compiler_params=_cparams(("arbitrary",)),
        name="moe_experts_up",
    )(tile_expert, tile_src, n_used, xs, w_gate, w_up)


def _experts_down_kernel(te_ref, nu_ref, hid_ref, wd_ref, y_ref, wd16_ref):
    i = pl.program_id(0)

    @pl.when(_expert_changed(te_ref, i))
    def _():
        wd16_ref[...] = wd_ref[...].astype(BF16)

    @pl.when(i < nu_ref[0])
    def _():
        y = jnp.dot(hid_ref[...], wd16_ref[...], preferred_element_type=F32)
        y_ref[...] = _pack_halves(y)

    @pl.when(i >= nu_ref[0])
    def _():
        y_ref[...] = jnp.zeros_like(y_ref)


def _experts_down(hid, w_down, tile_expert, n_used, tme):
    R, F = hid.shape
    NE, _, D = w_down.shape
    nt = R // tme
    grid_spec = pltpu.PrefetchScalarGridSpec(
        num_scalar_prefetch=2,
        grid=(nt,),
        in_specs=[pl.BlockSpec((tme, F), lambda i, te, nu: (i, 0)),
                  pl.BlockSpec((None, F, D), lambda i, te, nu: (te[i], 0, 0))],
        out_specs=pl.BlockSpec((tme, D // 2), lambda i, te, nu: (i, 0)),
        scratch_shapes=[pltpu.VMEM((F, D), BF16)],
    )
    return pl.pallas_call(
        _experts_down_kernel,
        grid_spec=grid_spec,
        out_shape=jax.ShapeDtypeStruct((R, D // 2), jnp.uint32),
        compiler_params=_cparams(("arbitrary",)),
        name="moe_experts_down",
    )(tile_expert, n_used, hid, w_down)


def _combine_kernel(pos0_ref, pos1_ref, x_ref, cw_ref, g_ref, y_ref, o_ref, buf, sem, *, TM, final_norm):
    i = pl.program_id(0)
    n = pl.num_programs(0)
    D2 = x_ref.shape[1] // 2

    def copies(step, slot, r):
        t = step * TM + r
        return (pltpu.make_async_copy(y_ref.at[pl.ds(pos0_ref[t], 1)], buf.at[slot, 0, pl.ds(r, 1)], sem.at[slot, 0]),
                pltpu.make_async_copy(y_ref.at[pl.ds(pos1_ref[t], 1)], buf.at[slot, 1, pl.ds(r, 1)], sem.at[slot, 1]))

    def start_all(step, slot):
        def body(r, carry):
            for cp in copies(step, slot, r):
                cp.start()
            return carry
        lax.fori_loop(0, TM, body, 0, unroll=DMA_UNROLL)

    def wait_all(step, slot):
        def body(r, carry):
            for cp in copies(step, slot, r):
                cp.wait()
            return carry
        lax.fori_loop(0, TM, body, 0, unroll=DMA_UNROLL)

    slot = lax.rem(i, 2)

    @pl.when(i == 0)
    def _():
        start_all(0, 0)

    @pl.when(i + 1 < n)
    def _():
        start_all(i + 1, 1 - slot)

    wait_all(i, slot)

    RB = 8

    def rows_body(rb, carry):
        rows = pl.ds(pl.multiple_of(rb * RB, RB), RB)
        cw = cw_ref[rows, :]
        c0 = cw[:, 0:1]
        c1 = cw[:, 1:2]
        y0h, y0l = _unpack_halves(buf[slot, 0, rows, :])
        y1h, y1l = _unpack_halves(buf[slot, 1, rows, :])
        xh = x_ref[rows, :D2] + c0 * y0h + c1 * y1h
        xl = x_ref[rows, D2:] + c0 * y0l + c1 * y1l
        if final_norm:
            ssq = jnp.sum(xh * xh, axis=-1, keepdims=True) + jnp.sum(xl * xl, axis=-1, keepdims=True)
            rstd = lax.rsqrt(ssq * (1.0 / (2 * D2)) + RMS_EPS)
            xh = xh * rstd * g_ref[:, :D2]
            xl = xl * rstd * g_ref[:, D2:]
        o_ref[rows, :D2] = xh
        o_ref[rows, D2:] = xl
        return carry

    lax.fori_loop(0, TM // RB, rows_body, 0, unroll=4)


def _combine(x1, cw, y, pos0, pos1, g, final_norm):
    T, D = x1.shape
    TM = _tile(T, 256, 8)
    L = cw.shape[1]
    grid_spec = pltpu.PrefetchScalarGridSpec(
        num_scalar_prefetch=2,
        grid=(T // TM,),
        in_specs=[pl.BlockSpec((TM, D), lambda i, p0, p1: (i, 0)),
                  pl.BlockSpec((TM, L), lambda i, p0, p1: (i, 0)),
                  pl.BlockSpec((1, D), lambda i, p0, p1: (0, 0)),
                  pl.BlockSpec(memory_space=pl.ANY)],
        out_specs=pl.BlockSpec((TM, D), lambda i, p0, p1: (i, 0)),
        scratch_shapes=[pltpu.VMEM((2, 2, TM, D // 2), jnp.uint32),
                        pltpu.SemaphoreType.DMA((2, 2))],
    )
    return pl.pallas_call(
        functools.partial(_combine_kernel, TM=TM, final_norm=final_norm),
        grid_spec=grid_spec,
        out_shape=jax.ShapeDtypeStruct((T, D), F32),
        compiler_params=_cparams(("arbitrary",)),
        name="moe_combine",
    )(pos0, pos1, x1, cw, g.reshape(1, D), y)


def _moe(x1, norm_g, r_grp, r_grp_b, r_exp, r_exp_b, w_gate, w_up, w_down, final_g, final_norm):
    T, D = x1.shape
    NG, EPG = r_exp.shape[1], r_exp.shape[2]
    NE = NG * EPG
    F = w_gate.shape[-1]
    tme = 256
    h2, idx, cw, cnt = _router(x1, norm_g, r_grp, r_grp_b, r_exp, r_exp_b)

    counts = cnt[0, EXPERT_LANE0:EXPERT_LANE0 + NE].astype(jnp.int32)
    padded = ((counts + tme - 1) // tme) * tme
    ends = jnp.cumsum(padded)
    starts = ends - padded
    nt = (2 * T + NE * (tme - 1) + tme - 1) // tme
    n_used = (ends[-1] // tme).astype(jnp.int32)
    tile_ids = jnp.minimum(jnp.arange(nt, dtype=jnp.int32), n_used - 1)
    tile_expert = jnp.minimum(
        jnp.sum((ends[None, :] <= (tile_ids * tme)[:, None]).astype(jnp.int32), axis=1), NE - 1)
    starts_row = jnp.zeros((1, ROUTER_LANES), F32).at[0, EXPERT_LANE0:EXPERT_LANE0 + NE].set(starts.astype(F32))

    pos = _positions(idx, starts_row)
    pos0, pos1 = pos[:, 0], pos[:, 1]
    xs = _dispatch(h2, pos0, pos1, nt * tme)
    hid = _experts_up(xs, w_gate.reshape(NE, D, F), w_up.reshape(NE, D, F),
                      tile_expert, tile_ids, n_used.reshape(1), tme)
    y = _experts_down(hid, w_down.reshape(NE, F, D), tile_expert, n_used.reshape(1), tme)
    return _combine(x1, cw, y, pos0, pos1, final_g, final_norm)


def kernel(x, norm_mix, w_in, ssd_conv_w, ssd_conv_b, ssd_dt_bias, ssd_a_log, ssd_d, ssd_norm, ssd_w_out, conf_glu_b, conf_dw_w, conf_dw_b, conf_ln_g, conf_ln_b, conf_w_out, w_out, norm_ffn, router_group, router_group_b, router_expert, router_expert_b, expert_w_gate, expert_w_up, expert_w_down, norm_final):
    B, S, D = x.shape
    T = B * S
    depth = w_in.shape[0]
    d_ssd = ssd_norm.shape[1]
    d_xbc = ssd_conv_w.shape[2]
    H = ssd_a_log.shape[1]
    d_conf = conf_dw_w.shape[2]
    o_xbc = d_ssd
    o_dt = o_xbc + d_xbc
    o_glu = o_dt + H
    o_gate = o_glu + 2 * d_conf

    xf = x.reshape(T, D)
    for l in range(depth):
        wl = w_in[l]
        h = _rmsnorm(xf, norm_mix[l], BF16)
        zs = _proj_act(h, wl, 0, d_ssd, "silu", BF16, "proj_z")
        xbc = _proj_act(h, wl, o_xbc, d_xbc, "none", BF16, "proj_xbc")
        dt, dtT = _proj_dt(h, wl, o_dt, H, ssd_dt_bias[l])
        glu = _proj_glu(h, wl, o_glu, o_glu + d_conf, d_conf,
                        conf_glu_b[l, :d_conf], conf_glu_b[l, d_conf:], BF16)
        gates = _proj_act(h, wl, o_gate, 2 * D, "sigmoid", BF16, "proj_gates")

        yn = _ssd(xbc, zs, dt, dtT, ssd_conv_w[l], ssd_conv_b[l], ssd_a_log[l], ssd_d[l], ssd_norm[l],
                  batch=B, seq=S, d_ssd=d_ssd)
        uc = _conformer(glu, conf_dw_w[l], conf_dw_b[l], conf_ln_g[l], conf_ln_b[l], batch=B, seq=S)

        m1 = _out_gate(yn, ssd_w_out[l].astype(BF16), gates, 0, None, F32, "out_ssd")
        mixed = _out_gate(uc, conf_w_out[l].astype(BF16), gates, D, m1, BF16, "out_conf")
        x1 = _out_res(mixed, w_out[l].astype(BF16), xf)

        xf = _moe(x1, norm_ffn[l], router_group[l], router_group_b[l], router_expert[l],
                  router_expert_b[l], expert_w_gate[l], expert_w_up[l], expert_w_down[l],
                  norm_final, l == depth - 1)
    return xf.reshape(B, S, D)
```

```python
import functools

import jax
import jax.numpy as jnp
from jax import lax
from jax.experimental import pallas as pl
from jax.experimental.pallas import tpu as pltpu

SSD_GROUPS = 8
SSD_CHUNK = 128
RMS_EPS = 1e-6
LN_EPS = 1e-5
NEG_BIG = -1e30
LANES = 128
DMA_UNROLL = 8
COMBINE_SLOTS = 3
ROUTER_LANES = 128
EXPERT_LANE0 = 64
VMEM_LIMIT_BYTES = 56 * 1024 * 1024

F32 = jnp.float32
BF16 = jnp.bfloat16


def _cparams(semantics):
    return pltpu.CompilerParams(dimension_semantics=semantics,
                                vmem_limit_bytes=VMEM_LIMIT_BYTES)


def _tile(n, pref, mult=128):
    if n <= pref:
        return n
    t = (pref // mult) * mult
    while t >= mult:
        if n % t == 0:
            return t
        t -= mult
    return n


def _sigmoid(x):
    return 0.5 + 0.5 * jnp.tanh(0.5 * x)


def _silu(x):
    h = 0.5 * x
    return h + h * jnp.tanh(h)


def _softplus(x):
    return jnp.maximum(x, 0.0) + jnp.log(1.0 + jnp.exp(-jnp.abs(x)))


def _wcols(K, tn, col0):
    assert col0 % LANES == 0 and tn % LANES == 0
    return pl.BlockSpec((pl.Element(K), pl.Element(tn)),
                        lambda j, i: (0, (col0 // LANES + j * (tn // LANES)) * LANES))


def _proj_act_kernel(a_ref, w_ref, o_ref, w16_ref, *, act):
    @pl.when(pl.program_id(1) == 0)
    def _():
        w16_ref[...] = w_ref[...].astype(BF16)

    acc = jnp.dot(a_ref[...], w16_ref[...], preferred_element_type=F32)
    if act == "silu":
        acc = _silu(acc)
    elif act == "sigmoid":
        acc = _sigmoid(acc)
    o_ref[...] = acc.astype(o_ref.dtype)


def _proj_act(a, w, col0, n, act, out_dtype, name):
    M, K = a.shape
    tm, tn = _tile(M, 1024, 8), _tile(n, 512)
    return pl.pallas_call(
        functools.partial(_proj_act_kernel, act=act),
        grid=(n // tn, M // tm),
        in_specs=[pl.BlockSpec((tm, K), lambda j, i: (i, 0)),
                  _wcols(K, tn, col0)],
        out_specs=pl.BlockSpec((tm, tn), lambda j, i: (i, j)),
        out_shape=jax.ShapeDtypeStruct((M, n), out_dtype),
        scratch_shapes=[pltpu.VMEM((K, tn), BF16)],
        compiler_params=_cparams(("parallel", "arbitrary")),
        name=name,
    )(a, w)


def _proj_glu_kernel(a_ref, wu_ref, wg_ref, bu_ref, bg_ref, o_ref, wu16_ref, wg16_ref):
    @pl.when(pl.program_id(1) == 0)
    def _():
        wu16_ref[...] = wu_ref[...].astype(BF16)
        wg16_ref[...] = wg_ref[...].astype(BF16)

    a = a_ref[...]
    u = jnp.dot(a, wu16_ref[...], preferred_element_type=F32) + bu_ref[...]
    g = jnp.dot(a, wg16_ref[...], preferred_element_type=F32) + bg_ref[...]
    o_ref[...] = (u * _sigmoid(g)).astype(o_ref.dtype)


def _proj_glu(a, w, col_u, col_g, n, bu, bg, out_dtype):
    M, K = a.shape
    tm, tn = _tile(M, 1024, 8), _tile(n, 256)
    return pl.pallas_call(
        _proj_glu_kernel,
        grid=(n // tn, M // tm),
        in_specs=[pl.BlockSpec((tm, K), lambda j, i: (i, 0)),
                  _wcols(K, tn, col_u),
                  _wcols(K, tn, col_g),
                  pl.BlockSpec((1, tn), lambda j, i: (0, j)),
                  pl.BlockSpec((1, tn), lambda j, i: (0, j))],
        out_specs=pl.BlockSpec((tm, tn), lambda j, i: (i, j)),
        out_shape=jax.ShapeDtypeStruct((M, n), out_dtype),
        scratch_shapes=[pltpu.VMEM((K, tn), BF16), pltpu.VMEM((K, tn), BF16)],
        compiler_params=_cparams(("parallel", "arbitrary")),
        name="proj_glu",
    )(a, w, w, bu.reshape(1, n), bg.reshape(1, n))


def _norm_proj_dt_kernel(x_ref, g_ref, w_ref, b_ref, h_ref, dt_ref, dtT_ref):
    x = x_ref[...]
    ms = jnp.mean(x * x, axis=-1, keepdims=True)
    h = (x * lax.rsqrt(ms + RMS_EPS) * g_ref[...]).astype(BF16)
    h_ref[...] = h
    acc = jnp.dot(h, w_ref[...].astype(BF16), preferred_element_type=F32) + b_ref[...]
    dt = _softplus(acc)
    dt_ref[...] = dt
    dtT_ref[...] = dt.T


def _norm_proj_dt(x, g, w, col0, H, b):
    M, K = x.shape
    tm = _tile(M, 256, 128)
    return pl.pallas_call(
        _norm_proj_dt_kernel,
        grid=(M // tm,),
        in_specs=[pl.BlockSpec((tm, K), lambda i: (i, 0)),
                  pl.BlockSpec((1, K), lambda i: (0, 0)),
                  pl.BlockSpec((pl.Element(K), pl.Element(H)), lambda i: (0, col0)),
                  pl.BlockSpec((1, H), lambda i: (0, 0))],
        out_specs=[pl.BlockSpec((tm, K), lambda i: (i, 0)),
                   pl.BlockSpec((tm, H), lambda i: (i, 0)),
                   pl.BlockSpec((H, tm), lambda i: (0, i))],
        out_shape=[jax.ShapeDtypeStruct((M, K), BF16),
                   jax.ShapeDtypeStruct((M, H), F32),
                   jax.ShapeDtypeStruct((H, M), F32)],
        compiler_params=_cparams(("parallel",)),
        name="norm_proj_dt",
    )(x, g.reshape(1, K), w, b.reshape(1, H))


def _out_gate_kernel(*refs, has_prev):
    if has_prev:
        a_ref, w_ref, g_ref, p_ref, o_ref = refs
    else:
        a_ref, w_ref, g_ref, o_ref = refs
    acc = jnp.dot(a_ref[...], w_ref[...], preferred_element_type=F32)
    acc = acc * g_ref[...].astype(F32)
    if has_prev:
        acc = acc + p_ref[...].astype(F32)
    o_ref[...] = acc.astype(o_ref.dtype)


def _out_gate(a, w, gates, gate_col0, prev, out_dtype, name):
    M, K = a.shape
    N = w.shape[1]
    tm, tn = _tile(M, 512, 8), _tile(N, 512)
    goff = gate_col0 // tn
    in_specs = [pl.BlockSpec((tm, K), lambda j, i: (i, 0)),
                pl.BlockSpec((K, tn), lambda j, i: (0, j)),
                pl.BlockSpec((tm, tn), lambda j, i: (i, j + goff))]
    args = [a, w, gates]
    if prev is not None:
        in_specs.append(pl.BlockSpec((tm, tn), lambda j, i: (i, j)))
        args.append(prev)
    return pl.pallas_call(
        functools.partial(_out_gate_kernel, has_prev=prev is not None),
        grid=(N // tn, M // tm),
        in_specs=in_specs,
        out_specs=pl.BlockSpec((tm, tn), lambda j, i: (i, j)),
        out_shape=jax.ShapeDtypeStruct((M, N), out_dtype),
        compiler_params=_cparams(("parallel", "parallel")),
        name=name,
    )(*args)


def _out_res_kernel(a_ref, w_ref, r_ref, o_ref):
    acc = jnp.dot(a_ref[...], w_ref[...], preferred_element_type=F32)
    o_ref[...] = r_ref[...] + acc


def _out_res(a, w, res):
    M, K = a.shape
    N = w.shape[1]
    tm, tn = _tile(M, 512, 8), _tile(N, 1024)
    return pl.pallas_call(
        _out_res_kernel,
        grid=(N // tn, M // tm),
        in_specs=[pl.BlockSpec((tm, K), lambda j, i: (i, 0)),
                  pl.BlockSpec((K, tn), lambda j, i: (0, j)),
                  pl.BlockSpec((tm, tn), lambda j, i: (i, j))],
        out_specs=pl.BlockSpec((tm, tn), lambda j, i: (i, j)),
        out_shape=jax.ShapeDtypeStruct((M, N), F32),
        compiler_params=_cparams(("parallel", "parallel")),
        name="out_res",
    )(a, w, res)


def _split3(v):
    hi = v.astype(BF16)
    r1 = v - hi.astype(F32)
    mid = r1.astype(BF16)
    lo = (r1 - mid.astype(F32)).astype(BF16)
    return hi, mid, lo


def _ssd_kernel(xs_ref, b_ref, c_ref, wx_ref, wb_ref, wc_ref, bx_ref, bb_ref, bc_ref,
                dt_ref, dtT_ref, alog_ref, alogT_ref, z_ref, dskip_ref, ng_ref,
                o_ref, ext_ref, state_ref, y_ref, *, Q, Hg, P, N, K):
    W = Hg * P
    HALO = 8
    NPL = Hg // 2
    c = pl.program_id(2)

    @pl.when(c == 0)
    def _():
        ext_ref[:, 0:HALO, :] = jnp.zeros((NPL + 2, HALO, 2 * P), F32)
        state_ref[...] = jnp.zeros_like(state_ref)

    for p in range(NPL):
        ext_ref[p, HALO:HALO + Q, :] = xs_ref[:, p * 2 * P:(p + 1) * 2 * P].astype(F32)
    ext_ref[NPL, HALO:HALO + Q, :] = b_ref[...].astype(F32)
    ext_ref[NPL + 1, HALO:HALO + Q, :] = c_ref[...].astype(F32)

    def conv_silu(plane, w_ref, bias_ref, wlo):
        acc = bias_ref[:, wlo:wlo + 2 * P]
        for k in range(K):
            r0 = HALO - (K - 1) + k
            acc = acc + w_ref[k:k + 1, wlo:wlo + 2 * P] * ext_ref[plane, r0:r0 + Q, :]
        return _silu(acc)

    bm = conv_silu(NPL, wb_ref, bb_ref, 0)
    cm = conv_silu(NPL + 1, wc_ref, bc_ref, 0)

    dt = dt_ref[...]
    dtT = dtT_ref[...]
    a = dt * (-jnp.exp(alog_ref[...]))
    aT = dtT * (-jnp.exp(alogT_ref[...]))

    row = lax.broadcasted_iota(jnp.int32, (Q, Q), 0)
    col = lax.broadcasted_iota(jnp.int32, (Q, Q), 1)
    causal = col <= row
    tril = jnp.where(causal, 1.0, 0.0).astype(BF16)
    triu = jnp.where(row <= col, 1.0, 0.0).astype(BF16)
    acum = jnp.dot(jnp.concatenate([tril, tril, tril], axis=1),
                   jnp.concatenate(_split3(a), axis=0), preferred_element_type=F32)
    acumT = jnp.dot(jnp.concatenate(_split3(aT), axis=1),
                    jnp.concatenate([triu, triu, triu], axis=0), preferred_element_type=F32)

    bm16 = bm.astype(BF16)
    cm16 = cm.astype(BF16)
    cb = lax.dot_general(cm16, bm16, (((1,), (1,)), ((), ())), preferred_element_type=F32)
    bT = bm.T

    lane = lax.broadcasted_iota(jnp.int32, (1, 2 * P), 1)
    lo_half = lane < P
    zero16 = jnp.zeros((), BF16)

    ssq = jnp.zeros((Q, 1), F32)
    for q in range(Hg // 2):
        l0 = q * 2 * P
        x_pair = conv_silu(q, wx_ref, bx_ref, l0)
        x16 = x_pair.astype(BF16)
        rhs_x = jnp.concatenate([jnp.where(lo_half, x16, zero16),
                                 jnp.where(lo_half, zero16, x16)], axis=0)
        s_old = state_ref[q]
        s16 = s_old.astype(BF16)
        rhs_s = jnp.concatenate([jnp.where(lo_half, s16, zero16),
                                 jnp.where(lo_half, zero16, s16)], axis=0)
        l_parts, ec_parts, bw_parts, g_parts = [], [], [], []
        for hh in range(2):
            h = 2 * q + hh
            ai = jnp.broadcast_to(acum[:, h:h + 1], (Q, Q))
            aj = acumT[h:h + 1, :]
            dtj = dtT[h:h + 1, :]
            seg = jnp.where(causal, ai - aj, NEG_BIG)
            l_parts.append((jnp.exp(seg) * cb * dtj).astype(BF16))
            ec_parts.append((jnp.exp(ai) * cm).astype(BF16))
            a_last = acumT[h:h + 1, Q - 1:Q]
            w_end = jnp.exp(a_last - aj) * dtj
            bw_parts.append((bT * w_end).astype(BF16))
            g_parts.append(jnp.exp(a_last))
        lhs_y = jnp.concatenate(l_parts + ec_parts, axis=1)
        rhs_y = jnp.concatenate([rhs_x, rhs_s], axis=0)
        y = jnp.dot(lhs_y, rhs_y, preferred_element_type=F32)
        upd = jnp.dot(jnp.concatenate(bw_parts, axis=1), rhs_x,
                      preferred_element_type=F32)
        decay = jnp.where(lo_half, g_parts[0], g_parts[1])
        state_ref[q] = s_old * decay + upd

        y = y + dskip_ref[:, l0:l0 + 2 * P] * x_pair
        y = y * z_ref[:, l0:l0 + 2 * P].astype(F32)
        y_ref[:, l0:l0 + 2 * P] = y
        ssq = ssq + jnp.sum(y * y, axis=-1, keepdims=True)

    ext_ref[:, 0:HALO, :] = ext_ref[:, Q:Q + HALO, :]
    rstd = lax.rsqrt(ssq * (1.0 / W) + RMS_EPS)
    o_ref[...] = (y_ref[...] * rstd * ng_ref[...]).astype(o_ref.dtype)


def _ssd(xbc, zs, dt, dtT, conv_w, conv_b, a_log, d_skip, norm_g, *, batch, seq, d_ssd):
    T = batch * seq
    G = SSD_GROUPS
    Q = SSD_CHUNK
    H = a_log.shape[0]
    Hg = H // G
    P = d_ssd // H
    W = Hg * P
    N = (xbc.shape[1] - d_ssd) // (2 * G)
    K = conv_w.shape[0]
    assert 2 * P == 128 and N == 128 and Q == 128 and Hg % 2 == 0 and W % 128 == 0
    assert seq % Q == 0 and K - 1 <= 8
    nc = seq // Q
    nb_x = d_ssd // N

    dt_g = dt.reshape(T, G, Hg).transpose(1, 0, 2)
    alog_g = a_log.reshape(G, 1, Hg)
    alogT = a_log.reshape(H, 1)
    dskip = jnp.repeat(d_skip, P).reshape(1, d_ssd)
    cb2 = conv_b.reshape(1, -1)

    rowblk = lambda b, g, c: b * nc + c
    kern = functools.partial(_ssd_kernel, Q=Q, Hg=Hg, P=P, N=N, K=K)
    return pl.pallas_call(
        kern,
        grid=(batch, G, nc),
        in_specs=[
            pl.BlockSpec((Q, W), lambda b, g, c: (rowblk(b, g, c), g)),
            pl.BlockSpec((Q, N), lambda b, g, c: (rowblk(b, g, c), nb_x + g)),
            pl.BlockSpec((Q, N), lambda b, g, c: (rowblk(b, g, c), nb_x + G + g)),
            pl.BlockSpec((K, W), lambda b, g, c: (0, g)),
            pl.BlockSpec((K, N), lambda b, g, c: (0, nb_x + g)),
            pl.BlockSpec((K, N), lambda b, g, c: (0, nb_x + G + g)),
            pl.BlockSpec((1, W), lambda b, g, c: (0, g)),
            pl.BlockSpec((1, N), lambda b, g, c: (0, nb_x + g)),
            pl.BlockSpec((1, N), lambda b, g, c: (0, nb_x + G + g)),
            pl.BlockSpec((None, Q, Hg), lambda b, g, c: (g, rowblk(b, g, c), 0)),
            pl.BlockSpec((Hg, Q), lambda b, g, c: (g, rowblk(b, g, c))),
            pl.BlockSpec((None, 1, Hg), lambda b, g, c: (g, 0, 0)),
            pl.BlockSpec((Hg, 1), lambda b, g, c: (g, 0)),
            pl.BlockSpec((Q, W), lambda b, g, c: (rowblk(b, g, c), g)),
            pl.BlockSpec((1, W), lambda b, g, c: (0, g)),
            pl.BlockSpec((1, W), lambda b, g, c: (0, g)),
        ],
        out_specs=pl.BlockSpec((Q, W), lambda b, g, c: (rowblk(b, g, c), g)),
        out_shape=jax.ShapeDtypeStruct((T, d_ssd), BF16),
        scratch_shapes=[pltpu.VMEM((Hg // 2 + 2, Q + 8, 2 * P), F32),
                        pltpu.VMEM((Hg // 2, N, 2 * P), F32),
                        pltpu.VMEM((Q, W), F32)],
        compiler_params=_cparams(("parallel", "parallel", "arbitrary")),
        name="ssd_scan",
    )(xbc, xbc, xbc, conv_w, conv_w, conv_w, cb2, cb2, cb2,
      dt_g, dtT, alog_g, alogT, zs, dskip, norm_g.reshape(1, d_ssd))


def _conf_kernel(u_ref, w_ref, b_ref, g_ref, beta_ref, o_ref, ext_ref, conv_ref, *, TM, C, K, HALO):
    NCH = C // 128
    i = pl.program_id(1)

    @pl.when(i == 0)
    def _():
        ext_ref[:, 0:HALO, :] = jnp.zeros((NCH, HALO, 128), F32)

    for ch in range(NCH):
        ext_ref[ch, HALO:HALO + TM, :] = u_ref[:, ch * 128:(ch + 1) * 128].astype(F32)

    def chunk_body(ch, carry):
        s1, = carry
        acc = jnp.broadcast_to(b_ref[ch], (TM, 128))
        for k in range(K):
            r0 = HALO - (K - 1) + k
            acc = acc + w_ref[ch, k:k + 1, :] * ext_ref[ch, r0:r0 + TM, :]
        conv_ref[ch] = acc
        ext_ref[ch, 0:HALO, :] = ext_ref[ch, TM:TM + HALO, :]
        return (s1 + acc,)

    s1, = lax.fori_loop(0, NCH, chunk_body, (jnp.zeros((TM, 128), F32),))
    mu = jnp.sum(s1, axis=-1, keepdims=True) * (1.0 / C)

    def var_body(ch, s2):
        d = conv_ref[ch] - mu
        return s2 + d * d

    s2 = lax.fori_loop(0, NCH, var_body, jnp.zeros((TM, 128), F32))
    rstd = lax.rsqrt(jnp.sum(s2, axis=-1, keepdims=True) * (1.0 / C) + LN_EPS)
    for ch in range(NCH):
        v = (conv_ref[ch] - mu) * rstd * g_ref[ch] + beta_ref[ch]
        o_ref[:, ch * 128:(ch + 1) * 128] = _silu(v).astype(o_ref.dtype)


def _conformer(u, dw_w, dw_b, ln_g, ln_b, *, batch, seq):
    T, C = u.shape
    K = dw_w.shape[0]
    HALO = 32
    assert K - 1 <= HALO and C % 128 == 0
    TM = _tile(seq, 128, 32)
    nt = seq // TM
    NCH = C // 128
    w3 = dw_w.reshape(K, NCH, 128).transpose(1, 0, 2)
    chunked = lambda v: v.reshape(NCH, 1, 128)
    kern = functools.partial(_conf_kernel, TM=TM, C=C, K=K, HALO=HALO)
    return pl.pallas_call(
        kern,
        grid=(batch, nt),
        in_specs=[pl.BlockSpec((TM, C), lambda b, i: (b * nt + i, 0)),
                  pl.BlockSpec((NCH, K, 128), lambda b, i: (0, 0, 0)),
                  pl.BlockSpec((NCH, 1, 128), lambda b, i: (0, 0, 0)),
                  pl.BlockSpec((NCH, 1, 128), lambda b, i: (0, 0, 0)),
                  pl.BlockSpec((NCH, 1, 128), lambda b, i: (0, 0, 0))],
        out_specs=pl.BlockSpec((TM, C), lambda b, i: (b * nt + i, 0)),
        out_shape=jax.ShapeDtypeStruct((T, C), BF16),
        scratch_shapes=[pltpu.VMEM((NCH, HALO + TM, 128), F32),
                        pltpu.VMEM((NCH, TM, 128), F32)],
        compiler_params=_cparams(("parallel", "arbitrary")),
        name="conformer",
    )(u, w3, chunked(dw_b), chunked(ln_g), chunked(ln_b))


def _pack_halves(v):
    d2 = v.shape[1] // 2
    hi = lax.bitcast_convert_type(v[:, :d2].astype(BF16).astype(F32), jnp.uint32)
    lo = lax.bitcast_convert_type(v[:, d2:].astype(BF16).astype(F32), jnp.uint32)
    return hi | (lo >> 16)


def _unpack_halves(p):
    hi = lax.bitcast_convert_type(p & jnp.uint32(0xFFFF0000), F32)
    lo = lax.bitcast_convert_type(p << 16, F32)
    return hi, lo


def _router_kernel(x_ref, g_ref, w2_ref, whi_ref, b_ref, h_ref, idx_ref, cw_ref, cnt_ref, *, TM, NG, EPG):
    i = pl.program_id(0)

    @pl.when(i == 0)
    def _():
        cnt_ref[...] = jnp.zeros_like(cnt_ref)

    x = x_ref[...]
    ms = jnp.mean(x * x, axis=-1, keepdims=True)
    h = x * lax.rsqrt(ms + RMS_EPS) * g_ref[...]
    h_ref[...] = _pack_halves(h)

    h_hi = h.astype(BF16)
    h_lo = (h - h_hi.astype(F32)).astype(BF16)
    p2 = jnp.dot(h_hi, w2_ref[...], preferred_element_type=F32)
    p1 = jnp.dot(h_lo, whi_ref[...], preferred_element_type=F32)
    L = ROUTER_LANES
    lg = p2[:, 0:L] + p2[:, L:2 * L] + p1 + b_ref[...]

    lane_i = lax.broadcasted_iota(jnp.int32, (TM, L), 1)
    lane = lane_i.astype(F32)
    gl = jnp.where(lane_i < NG, lg, NEG_BIG)
    gmax = jnp.max(gl, axis=-1, keepdims=True)
    gidx = jnp.min(jnp.where(gl == gmax, lane, float(L)), axis=-1, keepdims=True)
    p_top = 1.0 / jnp.sum(jnp.exp(gl - gmax), axis=-1, keepdims=True)

    egrp = jnp.floor((lane - EXPERT_LANE0) * (1.0 / EPG))
    sel = jnp.logical_and(lane_i >= EXPERT_LANE0, egrp == gidx)
    el = jnp.where(sel, lg, NEG_BIG)
    v0 = jnp.max(el, axis=-1, keepdims=True)
    i0 = jnp.min(jnp.where(el == v0, lane, float(L)), axis=-1, keepdims=True)
    el1 = jnp.where(lane == i0, NEG_BIG, el)
    v1 = jnp.max(el1, axis=-1, keepdims=True)
    i1 = jnp.min(jnp.where(el1 == v1, lane, float(L)), axis=-1, keepdims=True)
    t = jnp.exp(v1 - v0)
    w0 = 1.0 / (1.0 + t)
    c0 = p_top * w0
    c1 = p_top * (t * w0)

    oh0 = lane == i0
    oh1 = lane == i1
    oh = jnp.where(jnp.logical_or(oh0, oh1), 1.0, 0.0)
    r = lax.broadcasted_iota(jnp.int32, (TM, TM), 0)
    cidx = lax.broadcasted_iota(jnp.int32, (TM, TM), 1)
    strict = jnp.where(cidx < r, 1.0, 0.0).astype(BF16)
    before = jnp.dot(strict, oh.astype(BF16), preferred_element_type=F32) + cnt_ref[...]
    rank0 = jnp.sum(jnp.where(oh0, before, 0.0), axis=-1, keepdims=True)
    rank1 = jnp.sum(jnp.where(oh1, before, 0.0), axis=-1, keepdims=True)
    cnt_ref[...] = cnt_ref[...] + jnp.sum(oh, axis=0, keepdims=True)

    e0 = i0 - EXPERT_LANE0
    e1 = i1 - EXPERT_LANE0
    idx = jnp.where(lane_i == 0, e0, jnp.where(lane_i == 1, e1,
          jnp.where(lane_i == 2, rank0, jnp.where(lane_i == 3, rank1, 0.0))))
    idx_ref[...] = idx.astype(jnp.int32)
    cw_ref[...] = jnp.where(lane_i == 0, c0, jnp.where(lane_i == 1, c1, 0.0))


def _router(x1, g, r_grp, r_grp_b, r_exp, r_exp_b):
    T, D = x1.shape
    NG = r_grp.shape[1]
    EPG = r_exp.shape[2]
    NE = NG * EPG
    L = ROUTER_LANES
    assert NG <= EXPERT_LANE0 and EXPERT_LANE0 + NE <= L
    w = jnp.zeros((D, L), F32)
    w = w.at[:, 0:NG].set(r_grp).at[:, EXPERT_LANE0:EXPERT_LANE0 + NE].set(r_exp.reshape(D, NE))
    b = jnp.zeros((1, L), F32)
    b = b.at[0, 0:NG].set(r_grp_b).at[0, EXPERT_LANE0:EXPERT_LANE0 + NE].set(r_exp_b.reshape(NE))
    w_hi = w.astype(BF16)
    w_lo = (w - w_hi.astype(F32)).astype(BF16)
    w2 = jnp.concatenate([w_hi, w_lo], axis=1)
    TM = _tile(T, 256, 8)
    kern = functools.partial(_router_kernel, TM=TM, NG=NG, EPG=EPG)
    return pl.pallas_call(
        kern,
        grid=(T // TM,),
        in_specs=[pl.BlockSpec((TM, D), lambda i: (i, 0)),
                  pl.BlockSpec((1, D), lambda i: (0, 0)),
                  pl.BlockSpec((D, 2 * L), lambda i: (0, 0)),
                  pl.BlockSpec((D, L), lambda i: (0, 0)),
                  pl.BlockSpec((1, L), lambda i: (0, 0))],
        out_specs=[pl.BlockSpec((TM, D // 2), lambda i: (i, 0)),
                   pl.BlockSpec((TM, L), lambda i: (i, 0)),
                   pl.BlockSpec((TM, L), lambda i: (i, 0)),
                   pl.BlockSpec((1, L), lambda i: (0, 0))],
        out_shape=[jax.ShapeDtypeStruct((T, D // 2), jnp.uint32),
                   jax.ShapeDtypeStruct((T, L), jnp.int32),
                   jax.ShapeDtypeStruct((T, L), F32),
                   jax.ShapeDtypeStruct((1, L), F32)],
        compiler_params=_cparams(("arbitrary",)),
        name="router",
    )(x1, g.reshape(1, D), w2, w_hi, b)


def _positions_kernel(idx_ref, starts_ref, pos_ref):
    idx = idx_ref[...]
    lane = lax.broadcasted_iota(jnp.int32, idx.shape, 1)
    starts = starts_ref[...]

    def lookup(e):
        return jnp.sum(jnp.where(lane == e + EXPERT_LANE0, starts, 0.0), axis=-1, keepdims=True)

    pos0 = lookup(idx[:, 0:1]).astype(jnp.int32) + idx[:, 2:3]
    pos1 = lookup(idx[:, 1:2]).astype(jnp.int32) + idx[:, 3:4]
    pos_ref[...] = jnp.where(lane == 0, pos0, jnp.where(lane == 1, pos1, 0))


def _positions(idx, starts_row):
    T, L = idx.shape
    TM = _tile(T, 1024, 8)
    return pl.pallas_call(
        _positions_kernel,
        grid=(T // TM,),
        in_specs=[pl.BlockSpec((TM, L), lambda i: (i, 0)),
                  pl.BlockSpec((1, L), lambda i: (0, 0))],
        out_specs=pl.BlockSpec((TM, L), lambda i: (i, 0)),
        out_shape=jax.ShapeDtypeStruct((T, L), jnp.int32),
        compiler_params=_cparams(("parallel",)),
        name="moe_positions",
    )(idx, starts_row)


def _dispatch_kernel(pos0_ref, pos1_ref, h_ref, xs_in_ref, xs_ref, sem, *, TM):
    del xs_in_ref
    base = pl.program_id(0) * TM

    def copies(r):
        src = h_ref.at[pl.ds(r, 1)]
        return (pltpu.make_async_copy(src, xs_ref.at[pl.ds(pos0_ref[base + r], 1)], sem.at[0]),
                pltpu.make_async_copy(src, xs_ref.at[pl.ds(pos1_ref[base + r], 1)], sem.at[1]))

    def start(r, carry):
        for cp in copies(r):
            cp.start()
        return carry

    def wait(r, carry):
        for cp in copies(r):
            cp.wait()
        return carry

    lax.fori_loop(0, TM, start, 0, unroll=DMA_UNROLL)
    lax.fori_loop(0, TM, wait, 0, unroll=DMA_UNROLL)


def _dispatch(h2, pos0, pos1, rows):
    T, D2 = h2.shape
    TM = _tile(T, 256, 8)
    grid_spec = pltpu.PrefetchScalarGridSpec(
        num_scalar_prefetch=2,
        grid=(T // TM,),
        in_specs=[pl.BlockSpec((TM, D2), lambda i, p0, p1: (i, 0)),
                  pl.BlockSpec(memory_space=pl.ANY)],
        out_specs=pl.BlockSpec(memory_space=pl.ANY),
        scratch_shapes=[pltpu.SemaphoreType.DMA((2,))],
    )
    return pl.pallas_call(
        functools.partial(_dispatch_kernel, TM=TM),
        grid_spec=grid_spec,
        out_shape=jax.ShapeDtypeStruct((rows, D2), h2.dtype),
        input_output_aliases={3: 0},
        compiler_params=_cparams(("arbitrary",)),
        name="moe_dispatch",
    )(pos0, pos1, h2, jnp.zeros((rows, D2), h2.dtype))


def _expert_weight_copies(hbm_refs, stage_refs, sem, e, s):
    return [pltpu.make_async_copy(h.at[e], st.at[s], sem.at[s, k])
            for k, (h, st) in enumerate(zip(hbm_refs, stage_refs))]


def _stage_expert_weights(te_ref, first_ref, slot_ref, next_ref, hbm_refs, stage_refs, w16_refs, sem):
    i = pl.program_id(0)

    @pl.when(i == 0)
    def _():
        for cp in _expert_weight_copies(hbm_refs, stage_refs, sem, te_ref[0], 0):
            cp.start()

    @pl.when(first_ref[i] == 1)
    def _():
        s = slot_ref[i]

        @pl.when(next_ref[i] >= 0)
        def _():
            for cp in _expert_weight_copies(hbm_refs, stage_refs, sem, next_ref[i], 1 - s):
                cp.start()

        for cp in _expert_weight_copies(hbm_refs, stage_refs, sem, te_ref[i], s):
            cp.wait()
        for st, w16 in zip(stage_refs, w16_refs):
            w16[...] = st[s].astype(BF16)


def _experts_up_kernel(te_ref, ts_ref, first_ref, slot_ref, next_ref, nu_ref,
                       x_ref, wg_hbm, wu_hbm, hid_ref, wg_st, wu_st, wg16_ref, wu16_ref, sem):
    del ts_ref
    i = pl.program_id(0)
    _stage_expert_weights(te_ref, first_ref, slot_ref, next_ref,
                          (wg_hbm, wu_hbm), (wg_st, wu_st), (wg16_ref, wu16_ref), sem)

    @pl.when(i < nu_ref[0])
    def _():
        hi, lo = _unpack_halves(x_ref[...])
        x = jnp.concatenate([hi.astype(BF16), lo.astype(BF16)], axis=1)
        g = jnp.dot(x, wg16_ref[...], preferred_element_type=F32)
        u = jnp.dot(x, wu16_ref[...], preferred_element_type=F32)
        hid_ref[...] = (_silu(g) * u).astype(hid_ref.dtype)

    @pl.when(i >= nu_ref[0])
    def _():
        hid_ref[...] = jnp.zeros_like(hid_ref)


def _experts_up(xs, w_gate, w_up, plan, tme):
    R, D2 = xs.shape
    NE, D, F = w_gate.shape
    nt = R // tme
    nsp = len(plan)
    grid_spec = pltpu.PrefetchScalarGridSpec(
        num_scalar_prefetch=nsp,
        grid=(nt,),
        in_specs=[pl.BlockSpec((tme, D2), lambda i, te, ts, *_: (ts[i], 0)),
                  pl.BlockSpec(memory_space=pl.ANY),
                  pl.BlockSpec(memory_space=pl.ANY)],
        out_specs=pl.BlockSpec((tme, F), lambda i, *_: (i, 0)),
        scratch_shapes=[pltpu.VMEM((2, D, F), F32), pltpu.VMEM((2, D, F), F32),
                        pltpu.VMEM((D, F), BF16), pltpu.VMEM((D, F), BF16),
                        pltpu.SemaphoreType.DMA((2, 2))],
    )
    return pl.pallas_call(
        _experts_up_kernel,
        grid_spec=grid_spec,
        out_shape=jax.ShapeDtypeStruct((R, F), BF16),
        compiler_params=_cparams(("arbitrary",)),
        name="moe_experts_up",
    )(*plan, xs, w_gate, w_up)


def _experts_down_kernel(te_ref, ts_ref, first_ref, slot_ref, next_ref, nu_ref,
                         hid_ref, wd_hbm, y_ref, wd_st, wd16_ref, sem):
    del ts_ref
    i = pl.program_id(0)
    _stage_expert_weights(te_ref, first_ref, slot_ref, next_ref, (wd_hbm,), (wd_st,), (wd16_ref,), sem)

    @pl.when(i < nu_ref[0])
    def _():
        y = jnp.dot(hid_ref[...], wd16_ref[...], preferred_element_type=F32)
        y_ref[...] = _pack_halves(y)

    @pl.when(i >= nu_ref[0])
    def _():
        y_ref[...] = jnp.zeros_like(y_ref)


def _experts_down(hid, w_down, plan, tme):
    R, F = hid.shape
    NE, _, D = w_down.shape
    nt = R // tme
    grid_spec = pltpu.PrefetchScalarGridSpec(
        num_scalar_prefetch=len(plan),
        grid=(nt,),
        in_specs=[pl.BlockSpec((tme, F), lambda i, *_: (i, 0)),
                  pl.BlockSpec(memory_space=pl.ANY)],
        out_specs=pl.BlockSpec((tme, D // 2), lambda i, *_: (i, 0)),
        scratch_shapes=[pltpu.VMEM((2, F, D), F32), pltpu.VMEM((F, D), BF16),
                        pltpu.SemaphoreType.DMA((2, 1))],
    )
    return pl.pallas_call(
        _experts_down_kernel,
        grid_spec=grid_spec,
        out_shape=jax.ShapeDtypeStruct((R, D // 2), jnp.uint32),
        compiler_params=_cparams(("arbitrary",)),
        name="moe_experts_down",
    )(*plan, hid, w_down)


def _combine_kernel(pos0_ref, pos1_ref, x_ref, cw_ref, g_ref, y_ref, o_ref, buf, sem, *, TM, final_norm):
    i = pl.program_id(0)
    n = pl.num_programs(0)
    D2 = x_ref.shape[1] // 2

    def copies(step, slot, r):
        t = step * TM + r
        return (pltpu.make_async_copy(y_ref.at[pl.ds(pos0_ref[t], 1)], buf.at[slot, 0, pl.ds(r, 1)], sem.at[slot, 0]),
                pltpu.make_async_copy(y_ref.at[pl.ds(pos1_ref[t], 1)], buf.at[slot, 1, pl.ds(r, 1)], sem.at[slot, 1]))

    def start_all(step, slot):
        def body(r, carry):
            for cp in copies(step, slot, r):
                cp.start()
            return carry
        lax.fori_loop(0, TM, body, 0, unroll=DMA_UNROLL)

    def wait_all(step, slot):
        def body(r, carry):
            for cp in copies(step, slot, r):
                cp.wait()
            return carry
        lax.fori_loop(0, TM, body, 0, unroll=DMA_UNROLL)

    slot = lax.rem(i, COMBINE_SLOTS)
    slot_ahead = lax.rem(i + 2, COMBINE_SLOTS)

    @pl.when(i == 0)
    def _():
        start_all(0, 0)

        @pl.when(n > 1)
        def _():
            start_all(1, 1)

    wait_all(i, slot)

    RB = 8

    def rows_body(issue_ahead):
        def body(rb, carry):
            if issue_ahead:
                for r in range(RB):
                    for cp in copies(i + 2, slot_ahead, rb * RB + r):
                        cp.start()
            rows = pl.ds(pl.multiple_of(rb * RB, RB), RB)
            cw = cw_ref[rows, :]
            c0 = cw[:, 0:1]
            c1 = cw[:, 1:2]
            y0h, y0l = _unpack_halves(buf[slot, 0, rows, :])
            y1h, y1l = _unpack_halves(buf[slot, 1, rows, :])
            xh = x_ref[rows, :D2] + c0 * y0h + c1 * y1h
            xl = x_ref[rows, D2:] + c0 * y0l + c1 * y1l
            if final_norm:
                ssq = jnp.sum(xh * xh, axis=-1, keepdims=True) + jnp.sum(xl * xl, axis=-1, keepdims=True)
                rstd = lax.rsqrt(ssq * (1.0 / (2 * D2)) + RMS_EPS)
                xh = xh * rstd * g_ref[:, :D2]
                xl = xl * rstd * g_ref[:, D2:]
            o_ref[rows, :D2] = xh
            o_ref[rows, D2:] = xl
            return carry
        return body

    @pl.when(i + 2 < n)
    def _():
        lax.fori_loop(0, TM // RB, rows_body(True), 0, unroll=4)

    @pl.when(i + 2 >= n)
    def _():
        lax.fori_loop(0, TM // RB, rows_body(False), 0, unroll=4)


def _combine(x1, cw, y, pos0, pos1, g, final_norm):
    T, D = x1.shape
    TM = _tile(T, 256, 8)
    L = cw.shape[1]
    grid_spec = pltpu.PrefetchScalarGridSpec(
        num_scalar_prefetch=2,
        grid=(T // TM,),
        in_specs=[pl.BlockSpec((TM, D), lambda i, p0, p1: (i, 0)),
                  pl.BlockSpec((TM, L), lambda i, p0, p1: (i, 0)),
                  pl.BlockSpec((1, D), lambda i, p0, p1: (0, 0)),
                  pl.BlockSpec(memory_space=pl.ANY)],
        out_specs=pl.BlockSpec((TM, D), lambda i, p0, p1: (i, 0)),
        scratch_shapes=[pltpu.VMEM((COMBINE_SLOTS, 2, TM, D // 2), jnp.uint32),
                        pltpu.SemaphoreType.DMA((COMBINE_SLOTS, 2))],
    )
    return pl.pallas_call(
        functools.partial(_combine_kernel, TM=TM, final_norm=final_norm),
        grid_spec=grid_spec,
        out_shape=jax.ShapeDtypeStruct((T, D), F32),
        compiler_params=_cparams(("arbitrary",)),
        name="moe_combine",
    )(pos0, pos1, x1, cw, g.reshape(1, D), y)


def _moe(x1, norm_g, r_grp, r_grp_b, r_exp, r_exp_b, w_gate, w_up, w_down, final_g, final_norm):
    T, D = x1.shape
    NG, EPG = r_exp.shape[1], r_exp.shape[2]
    NE = NG * EPG
    F = w_gate.shape[-1]
    tme = 256
    h2, idx, cw, cnt = _router(x1, norm_g, r_grp, r_grp_b, r_exp, r_exp_b)

    counts = cnt[0, EXPERT_LANE0:EXPERT_LANE0 + NE].astype(jnp.int32)
    padded = ((counts + tme - 1) // tme) * tme
    ends = jnp.cumsum(padded)
    starts = ends - padded
    nt = (2 * T + NE * (tme - 1) + tme - 1) // tme
    n_used = (ends[-1] // tme).astype(jnp.int32)
    tile_ids = jnp.minimum(jnp.arange(nt, dtype=jnp.int32), n_used - 1)
    tile_expert = jnp.minimum(
        jnp.sum((ends[None, :] <= (tile_ids * tme)[:, None]).astype(jnp.int32), axis=1), NE - 1)
    starts_row = jnp.zeros((1, ROUTER_LANES), F32).at[0, EXPERT_LANE0:EXPERT_LANE0 + NE].set(starts.astype(F32))
    ids = jnp.arange(NE, dtype=jnp.int32)
    later = jnp.logical_and(ids[None, :] > ids[:, None], (counts > 0)[None, :])
    next_expert = jnp.min(jnp.where(later, ids[None, :], NE), axis=1)
    next_expert = jnp.where(next_expert == NE, -1, next_expert).astype(jnp.int32)
    prev_expert = jnp.concatenate([jnp.full((1,), -1, jnp.int32), tile_expert[:-1]])
    first = jnp.logical_and(tile_expert != prev_expert, jnp.arange(nt) < n_used).astype(jnp.int32)
    slot = ((jnp.cumsum(first) - 1) % 2).astype(jnp.int32)
    plan = (tile_expert, tile_ids, first, slot, next_expert[tile_expert], n_used.reshape(1))

    pos = _positions(idx, starts_row)
    pos0, pos1 = pos[:, 0], pos[:, 1]
    xs = _dispatch(h2, pos0, pos1, nt * tme)
    hid = _experts_up(xs, w_gate.reshape(NE, D, F), w_up.reshape(NE, D, F), plan, tme)
    y = _experts_down(hid, w_down.reshape(NE, F, D), plan, tme)
    return _combine(x1, cw, y, pos0, pos1, final_g, final_norm)


def kernel(x, norm_mix, w_in, ssd_conv_w, ssd_conv_b, ssd_dt_bias, ssd_a_log, ssd_d, ssd_norm, ssd_w_out, conf_glu_b, conf_dw_w, conf_dw_b, conf_ln_g, conf_ln_b, conf_w_out, w_out, norm_ffn, router_group, router_group_b, router_expert, router_expert_b, expert_w_gate, expert_w_up, expert_w_down, norm_final):
    B, S, D = x.shape
    T = B * S
    depth = w_in.shape[0]
    d_ssd = ssd_norm.shape[1]
    d_xbc = ssd_conv_w.shape[2]
    H = ssd_a_log.shape[1]
    d_conf = conf_dw_w.shape[2]
    o_xbc = d_ssd
    o_dt = o_xbc + d_xbc
    o_glu = o_dt + H
    o_gate = o_glu + 2 * d_conf

    xf = x.reshape(T, D)
    for l in range(depth):
        wl = w_in[l]
        h, dt, dtT = _norm_proj_dt(xf, norm_mix[l], wl, o_dt, H, ssd_dt_bias[l])
        zs = _proj_act(h, wl, 0, d_ssd, "silu", BF16, "proj_z")
        xbc = _proj_act(h, wl, o_xbc, d_xbc, "none", BF16, "proj_xbc")
        glu = _proj_glu(h, wl, o_glu, o_glu + d_conf, d_conf,
                        conf_glu_b[l, :d_conf], conf_glu_b[l, d_conf:], BF16)
        gates = _proj_act(h, wl, o_gate, 2 * D, "sigmoid", BF16, "proj_gates")

        yn = _ssd(xbc, zs, dt, dtT, ssd_conv_w[l], ssd_conv_b[l], ssd_a_log[l], ssd_d[l], ssd_norm[l],
                  batch=B, seq=S, d_ssd=d_ssd)
        uc = _conformer(glu, conf_dw_w[l], conf_dw_b[l], conf_ln_g[l], conf_ln_b[l], batch=B, seq=S)

        m1 = _out_gate(yn, ssd_w_out[l].astype(BF16), gates, 0, None, F32, "out_ssd")
        mixed = _out_gate(uc, conf_w_out[l].astype(BF16), gates, D, m1, BF16, "out_conf")
        x1 = _out_res(mixed, w_out[l].astype(BF16), xf)

        xf = _moe(x1, norm_ffn[l], router_group[l], router_group_b[l], router_expert[l],
                  router_expert_b[l], expert_w_gate[l], expert_w_up[l], expert_w_down[l],
                  norm_final, l == depth - 1)
    return xf.reshape(B, S, D)
```

```python
import functools

import jax
import jax.numpy as jnp
from jax import lax
from jax.experimental import pallas as pl
from jax.experimental.pallas import tpu as pltpu

SSD_GROUPS = 8
SSD_CHUNK = 128
SSD_CHUNKS_PER_STEP = 2
RMS_EPS = 1e-6
LN_EPS = 1e-5
NEG_BIG = -1e30
LOG2E = 1.4426950408889634
LANES = 128
DMA_UNROLL = 8
COMBINE_SLOTS = 3
ROUTER_LANES = 128
EXPERT_LANE0 = 64
VMEM_LIMIT_BYTES = 56 * 1024 * 1024

F32 = jnp.float32
BF16 = jnp.bfloat16


def _cparams(semantics):
    return pltpu.CompilerParams(dimension_semantics=semantics,
                                vmem_limit_bytes=VMEM_LIMIT_BYTES)


def _tile(n, pref, mult=128):
    if n <= pref:
        return n
    t = (pref // mult) * mult
    while t >= mult:
        if n % t == 0:
            return t
        t -= mult
    return n


def _sigmoid(x):
    return 0.5 + 0.5 * jnp.tanh(0.5 * x)


def _silu(x):
    h = 0.5 * x
    return h + h * jnp.tanh(h)


def _softplus(x):
    return jnp.maximum(x, 0.0) + jnp.log(1.0 + jnp.exp(-jnp.abs(x)))


def _wcols(K, tn, col0):
    assert col0 % LANES == 0 and tn % LANES == 0
    return pl.BlockSpec((pl.Element(K), pl.Element(tn)),
                        lambda j, i: (0, (col0 // LANES + j * (tn // LANES)) * LANES))


def _proj_act_kernel(a_ref, w_ref, o_ref, w16_ref, *, act):
    @pl.when(pl.program_id(1) == 0)
    def _():
        w16_ref[...] = w_ref[...].astype(BF16)

    acc = jnp.dot(a_ref[...], w16_ref[...], preferred_element_type=F32)
    if act == "silu":
        acc = _silu(acc)
    elif act == "sigmoid":
        acc = _sigmoid(acc)
    o_ref[...] = acc.astype(o_ref.dtype)


def _proj_act(a, w, col0, n, act, out_dtype, name):
    M, K = a.shape
    tm, tn = _tile(M, 512, 8), _tile(n, 1024)
    return pl.pallas_call(
        functools.partial(_proj_act_kernel, act=act),
        grid=(n // tn, M // tm),
        in_specs=[pl.BlockSpec((tm, K), lambda j, i: (i, 0)),
                  _wcols(K, tn, col0)],
        out_specs=pl.BlockSpec((tm, tn), lambda j, i: (i, j)),
        out_shape=jax.ShapeDtypeStruct((M, n), out_dtype),
        scratch_shapes=[pltpu.VMEM((K, tn), BF16)],
        compiler_params=_cparams(("parallel", "arbitrary")),
        name=name,
    )(a, w)


def _proj_glu_kernel(a_ref, wu_ref, wg_ref, bu_ref, bg_ref, o_ref, wu16_ref, wg16_ref):
    @pl.when(pl.program_id(1) == 0)
    def _():
        wu16_ref[...] = wu_ref[...].astype(BF16)
        wg16_ref[...] = wg_ref[...].astype(BF16)

    a = a_ref[...]
    u = jnp.dot(a, wu16_ref[...], preferred_element_type=F32) + bu_ref[...]
    g = jnp.dot(a, wg16_ref[...], preferred_element_type=F32) + bg_ref[...]
    o_ref[...] = (u * _sigmoid(g)).astype(o_ref.dtype)


def _proj_glu(a, w, col_u, col_g, n, bu, bg, out_dtype):
    M, K = a.shape
    tm, tn = _tile(M, 1024, 8), _tile(n, 256)
    return pl.pallas_call(
        _proj_glu_kernel,
        grid=(n // tn, M // tm),
        in_specs=[pl.BlockSpec((tm, K), lambda j, i: (i, 0)),
                  _wcols(K, tn, col_u),
                  _wcols(K, tn, col_g),
                  pl.BlockSpec((1, tn), lambda j, i: (0, j)),
                  pl.BlockSpec((1, tn), lambda j, i: (0, j))],
        out_specs=pl.BlockSpec((tm, tn), lambda j, i: (i, j)),
        out_shape=jax.ShapeDtypeStruct((M, n), out_dtype),
        scratch_shapes=[pltpu.VMEM((K, tn), BF16), pltpu.VMEM((K, tn), BF16)],
        compiler_params=_cparams(("parallel", "arbitrary")),
        name="proj_glu",
    )(a, w, w, bu.reshape(1, n), bg.reshape(1, n))


def _norm_proj_dt_kernel(x_ref, g_ref, w_ref, b_ref, h_ref, dt_ref, dtT_ref):
    x = x_ref[...]
    ms = jnp.mean(x * x, axis=-1, keepdims=True)
    h = (x * lax.rsqrt(ms + RMS_EPS) * g_ref[...]).astype(BF16)
    h_ref[...] = h
    acc = jnp.dot(h, w_ref[...].astype(BF16), preferred_element_type=F32) + b_ref[...]
    dt = _softplus(acc)
    dt_ref[...] = dt
    dtT_ref[...] = dt.T


def _norm_proj_dt(x, g, w, col0, H, b):
    M, K = x.shape
    tm = _tile(M, 256, 128)
    return pl.pallas_call(
        _norm_proj_dt_kernel,
        grid=(M // tm,),
        in_specs=[pl.BlockSpec((tm, K), lambda i: (i, 0)),
                  pl.BlockSpec((1, K), lambda i: (0, 0)),
                  pl.BlockSpec((pl.Element(K), pl.Element(H)), lambda i: (0, col0)),
                  pl.BlockSpec((1, H), lambda i: (0, 0))],
        out_specs=[pl.BlockSpec((tm, K), lambda i: (i, 0)),
                   pl.BlockSpec((tm, H), lambda i: (i, 0)),
                   pl.BlockSpec((H, tm), lambda i: (0, i))],
        out_shape=[jax.ShapeDtypeStruct((M, K), BF16),
                   jax.ShapeDtypeStruct((M, H), F32),
                   jax.ShapeDtypeStruct((H, M), F32)],
        compiler_params=_cparams(("parallel",)),
        name="norm_proj_dt",
    )(x, g.reshape(1, K), w, b.reshape(1, H))


def _out_gate_kernel(*refs, has_prev):
    if has_prev:
        a_ref, w_ref, g_ref, p_ref, o_ref = refs
    else:
        a_ref, w_ref, g_ref, o_ref = refs
    acc = jnp.dot(a_ref[...], w_ref[...], preferred_element_type=F32)
    acc = acc * g_ref[...].astype(F32)
    if has_prev:
        acc = acc + p_ref[...].astype(F32)
    o_ref[...] = acc.astype(o_ref.dtype)


def _out_gate(a, w, gates, gate_col0, prev, out_dtype, name):
    M, K = a.shape
    N = w.shape[1]
    tm, tn = _tile(M, 512, 8), _tile(N, 512)
    goff = gate_col0 // tn
    in_specs = [pl.BlockSpec((tm, K), lambda j, i: (i, 0)),
                pl.BlockSpec((K, tn), lambda j, i: (0, j)),
                pl.BlockSpec((tm, tn), lambda j, i: (i, j + goff))]
    args = [a, w, gates]
    if prev is not None:
        in_specs.append(pl.BlockSpec((tm, tn), lambda j, i: (i, j)))
        args.append(prev)
    return pl.pallas_call(
        functools.partial(_out_gate_kernel, has_prev=prev is not None),
        grid=(N // tn, M // tm),
        in_specs=in_specs,
        out_specs=pl.BlockSpec((tm, tn), lambda j, i: (i, j)),
        out_shape=jax.ShapeDtypeStruct((M, N), out_dtype),
        compiler_params=_cparams(("parallel", "parallel")),
        name=name,
    )(*args)


def _out_res_kernel(a_ref, w_ref, r_ref, o_ref):
    acc = jnp.dot(a_ref[...], w_ref[...], preferred_element_type=F32)
    o_ref[...] = r_ref[...] + acc


def _out_res(a, w, res):
    M, K = a.shape
    N = w.shape[1]
    tm, tn = _tile(M, 512, 8), _tile(N, 1024)
    return pl.pallas_call(
        _out_res_kernel,
        grid=(N // tn, M // tm),
        in_specs=[pl.BlockSpec((tm, K), lambda j, i: (i, 0)),
                  pl.BlockSpec((K, tn), lambda j, i: (0, j)),
                  pl.BlockSpec((tm, tn), lambda j, i: (i, j))],
        out_specs=pl.BlockSpec((tm, tn), lambda j, i: (i, j)),
        out_shape=jax.ShapeDtypeStruct((M, N), F32),
        compiler_params=_cparams(("parallel", "parallel")),
        name="out_res",
    )(a, w, res)


def _split3(v):
    hi = v.astype(BF16)
    r1 = v - hi.astype(F32)
    mid = r1.astype(BF16)
    lo = (r1 - mid.astype(F32)).astype(BF16)
    return hi, mid, lo


def _ssd_kernel(xs_ref, b_ref, c_ref, wx_ref, wb_ref, wc_ref, bx_ref, bb_ref, bc_ref,
                dt_ref, dtT_ref, alog_ref, alogT_ref, z_ref, dskip_ref, ng_ref,
                o_ref, ext_ref, state_ref, y_ref, *, Q, Hg, P, N, K):
    W = Hg * P
    HALO = 8
    NPL = Hg // 2
    c = pl.program_id(2)

    @pl.when(c == 0)
    def _():
        ext_ref[:, 0:HALO, :] = jnp.zeros((NPL + 2, HALO, 2 * P), F32)
        state_ref[...] = jnp.zeros_like(state_ref)

    for r0 in range(0, xs_ref.shape[0], Q):
        _ssd_chunk(xs_ref, b_ref, c_ref, wx_ref, wb_ref, wc_ref, bx_ref, bb_ref, bc_ref,
                   dt_ref, dtT_ref, alog_ref, alogT_ref, z_ref, dskip_ref, ng_ref,
                   o_ref, ext_ref, state_ref, y_ref, r0, Q=Q, Hg=Hg, P=P, N=N, K=K)


def _ssd_chunk(xs_ref, b_ref, c_ref, wx_ref, wb_ref, wc_ref, bx_ref, bb_ref, bc_ref,
               dt_ref, dtT_ref, alog_ref, alogT_ref, z_ref, dskip_ref, ng_ref,
               o_ref, ext_ref, state_ref, y_ref, r0, *, Q, Hg, P, N, K):
    W = Hg * P
    HALO = 8
    NPL = Hg // 2
    rows = slice(r0, r0 + Q)

    for p in range(NPL):
        ext_ref[p, HALO:HALO + Q, :] = xs_ref[rows, p * 2 * P:(p + 1) * 2 * P].astype(F32)
    ext_ref[NPL, HALO:HALO + Q, :] = b_ref[rows, :].astype(F32)
    ext_ref[NPL + 1, HALO:HALO + Q, :] = c_ref[rows, :].astype(F32)

    def conv_silu(plane, w_ref, bias_ref, wlo):
        acc = bias_ref[:, wlo:wlo + 2 * P]
        for k in range(K):
            r0 = HALO - (K - 1) + k
            acc = acc + w_ref[k:k + 1, wlo:wlo + 2 * P] * ext_ref[plane, r0:r0 + Q, :]
        return acc + acc * jnp.tanh(acc)

    bm = conv_silu(NPL, wb_ref, bb_ref, 0)
    cm = conv_silu(NPL + 1, wc_ref, bc_ref, 0)

    dt = dt_ref[rows, :]
    dtT = dtT_ref[:, rows]
    a = dt * (-jnp.exp(alog_ref[...]) * LOG2E)
    aT = dtT * (-jnp.exp(alogT_ref[...]) * LOG2E)

    row = lax.broadcasted_iota(jnp.int32, (Q, Q), 0)
    col = lax.broadcasted_iota(jnp.int32, (Q, Q), 1)
    causal = col <= row
    tril = jnp.where(causal, 1.0, 0.0).astype(BF16)
    triu = jnp.where(row <= col, 1.0, 0.0).astype(BF16)
    acum = jnp.dot(jnp.concatenate([tril, tril, tril], axis=1),
                   jnp.concatenate(_split3(a), axis=0), preferred_element_type=F32)
    acumT = jnp.dot(jnp.concatenate(_split3(aT), axis=1),
                    jnp.concatenate([triu, triu, triu], axis=0), preferred_element_type=F32)

    bm16 = bm.astype(BF16)
    cm16 = cm.astype(BF16)
    cb = lax.dot_general(cm16, bm16, (((1,), (1,)), ((), ())), preferred_element_type=F32)
    bT = bm.T

    lane = lax.broadcasted_iota(jnp.int32, (1, 2 * P), 1)
    lo_half = lane < P
    zero16 = jnp.zeros((), BF16)

    ssq = jnp.zeros((Q, 1), F32)
    for q in range(Hg // 2):
        l0 = q * 2 * P
        x_pair = conv_silu(q, wx_ref, bx_ref, l0)
        x16 = x_pair.astype(BF16)
        rhs_x = jnp.concatenate([jnp.where(lo_half, x16, zero16),
                                 jnp.where(lo_half, zero16, x16)], axis=0)
        s_old = state_ref[q]
        s16 = s_old.astype(BF16)
        rhs_s = jnp.concatenate([jnp.where(lo_half, s16, zero16),
                                 jnp.where(lo_half, zero16, s16)], axis=0)
        l_parts, ec_parts, bw_parts, g_parts = [], [], [], []
        for hh in range(2):
            h = 2 * q + hh
            ai = jnp.broadcast_to(acum[:, h:h + 1], (Q, Q))
            aj = acumT[h:h + 1, :]
            dtj = dtT[h:h + 1, :]
            seg = jnp.where(causal, ai - aj, NEG_BIG)
            l_parts.append((jnp.exp2(seg) * cb * dtj).astype(BF16))
            ec_parts.append((jnp.exp2(ai) * cm).astype(BF16))
            a_last = acumT[h:h + 1, Q - 1:Q]
            w_end = jnp.exp2(a_last - aj) * dtj
            bw_parts.append((bT * w_end).astype(BF16))
            g_parts.append(jnp.exp2(a_last))
        lhs_y = jnp.concatenate(l_parts + ec_parts, axis=1)
        rhs_y = jnp.concatenate([rhs_x, rhs_s], axis=0)
        y = jnp.dot(lhs_y, rhs_y, preferred_element_type=F32)
        upd = jnp.dot(jnp.concatenate(bw_parts, axis=1), rhs_x,
                      preferred_element_type=F32)
        decay = jnp.where(lo_half, g_parts[0], g_parts[1])
        state_ref[q] = s_old * decay + upd

        y = y + dskip_ref[:, l0:l0 + 2 * P] * x_pair
        y = y * z_ref[rows, l0:l0 + 2 * P].astype(F32)
        y_ref[:, l0:l0 + 2 * P] = y
        ssq = ssq + jnp.sum(y * y, axis=-1, keepdims=True)

    ext_ref[:, 0:HALO, :] = ext_ref[:, Q:Q + HALO, :]
    rstd = lax.rsqrt(ssq * (1.0 / W) + RMS_EPS)
    o_ref[rows, :] = (y_ref[...] * rstd * ng_ref[...]).astype(o_ref.dtype)


def _ssd(xbc, zs, dt, dtT, conv_w, conv_b, a_log, d_skip, norm_g, *, batch, seq, d_ssd):
    T = batch * seq
    G = SSD_GROUPS
    Q = SSD_CHUNK
    H = a_log.shape[0]
    Hg = H // G
    P = d_ssd // H
    W = Hg * P
    N = (xbc.shape[1] - d_ssd) // (2 * G)
    K = conv_w.shape[0]
    assert 2 * P == 128 and N == 128 and Q == 128 and Hg % 2 == 0 and W % 128 == 0
    assert seq % Q == 0 and K - 1 <= 8
    QB = Q * SSD_CHUNKS_PER_STEP if seq % (Q * SSD_CHUNKS_PER_STEP) == 0 else Q
    nc = seq // QB
    nb_x = d_ssd // N

    dt_g = dt.reshape(T, G, Hg).transpose(1, 0, 2)
    alog_g = a_log.reshape(G, 1, Hg)
    alogT = a_log.reshape(H, 1)
    dskip = jnp.repeat(d_skip, P).reshape(1, d_ssd)
    conv_w = 0.5 * conv_w
    cb2 = (0.5 * conv_b).reshape(1, -1)

    rowblk = lambda b, g, c: b * nc + c
    kern = functools.partial(_ssd_kernel, Q=Q, Hg=Hg, P=P, N=N, K=K)
    return pl.pallas_call(
        kern,
        grid=(batch, G, nc),
        in_specs=[
            pl.BlockSpec((QB, W), lambda b, g, c: (rowblk(b, g, c), g)),
            pl.BlockSpec((QB, N), lambda b, g, c: (rowblk(b, g, c), nb_x + g)),
            pl.BlockSpec((QB, N), lambda b, g, c: (rowblk(b, g, c), nb_x + G + g)),
            pl.BlockSpec((K, W), lambda b, g, c: (0, g)),
            pl.BlockSpec((K, N), lambda b, g, c: (0, nb_x + g)),
            pl.BlockSpec((K, N), lambda b, g, c: (0, nb_x + G + g)),
            pl.BlockSpec((1, W), lambda b, g, c: (0, g)),
            pl.BlockSpec((1, N), lambda b, g, c: (0, nb_x + g)),
            pl.BlockSpec((1, N), lambda b, g, c: (0, nb_x + G + g)),
            pl.BlockSpec((None, QB, Hg), lambda b, g, c: (g, rowblk(b, g, c), 0)),
            pl.BlockSpec((Hg, QB), lambda b, g, c: (g, rowblk(b, g, c))),
            pl.BlockSpec((None, 1, Hg), lambda b, g, c: (g, 0, 0)),
            pl.BlockSpec((Hg, 1), lambda b, g, c: (g, 0)),
            pl.BlockSpec((QB, W), lambda b, g, c: (rowblk(b, g, c), g)),
            pl.BlockSpec((1, W), lambda b, g, c: (0, g)),
            pl.BlockSpec((1, W), lambda b, g, c: (0, g)),
        ],
        out_specs=pl.BlockSpec((QB, W), lambda b, g, c: (rowblk(b, g, c), g)),
        out_shape=jax.ShapeDtypeStruct((T, d_ssd), BF16),
        scratch_shapes=[pltpu.VMEM((Hg // 2 + 2, Q + 8, 2 * P), F32),
                        pltpu.VMEM((Hg // 2, N, 2 * P), F32),
                        pltpu.VMEM((Q, W), F32)],
        compiler_params=_cparams(("parallel", "parallel", "arbitrary")),
        name="ssd_scan",
    )(xbc, xbc, xbc, conv_w, conv_w, conv_w, cb2, cb2, cb2,
      dt_g, dtT, alog_g, alogT, zs, dskip, norm_g.reshape(1, d_ssd))


def _conf_kernel(u_ref, w_ref, b_ref, g_ref, beta_ref, o_ref, ext_ref, conv_ref, *, TM, C, K, HALO):
    NCH = C // 128
    i = pl.program_id(1)

    @pl.when(i == 0)
    def _():
        ext_ref[:, 0:HALO, :] = jnp.zeros((NCH, HALO, 128), F32)

    for ch in range(NCH):
        ext_ref[ch, HALO:HALO + TM, :] = u_ref[:, ch * 128:(ch + 1) * 128].astype(F32)

    def chunk_body(ch, carry):
        s1, = carry
        acc = jnp.broadcast_to(b_ref[ch], (TM, 128))
        for k in range(K):
            r0 = HALO - (K - 1) + k
            acc = acc + w_ref[ch, k:k + 1, :] * ext_ref[ch, r0:r0 + TM, :]
        conv_ref[ch] = acc
        ext_ref[ch, 0:HALO, :] = ext_ref[ch, TM:TM + HALO, :]
        return (s1 + acc,)

    s1, = lax.fori_loop(0, NCH, chunk_body, (jnp.zeros((TM, 128), F32),))
    mu = jnp.sum(s1, axis=-1, keepdims=True) * (1.0 / C)

    def var_body(ch, s2):
        d = conv_ref[ch] - mu
        return s2 + d * d

    s2 = lax.fori_loop(0, NCH, var_body, jnp.zeros((TM, 128), F32))
    rstd = lax.rsqrt(jnp.sum(s2, axis=-1, keepdims=True) * (1.0 / C) + LN_EPS)
    for ch in range(NCH):
        h = (conv_ref[ch] - mu) * rstd * g_ref[ch] + beta_ref[ch]
        o_ref[:, ch * 128:(ch + 1) * 128] = (h + h * jnp.tanh(h)).astype(o_ref.dtype)


def _conformer(u, dw_w, dw_b, ln_g, ln_b, *, batch, seq):
    T, C = u.shape
    K = dw_w.shape[0]
    HALO = 32
    assert K - 1 <= HALO and C % 128 == 0
    TM = _tile(seq, 128, 32)
    nt = seq // TM
    NCH = C // 128
    w3 = dw_w.reshape(K, NCH, 128).transpose(1, 0, 2)
    chunked = lambda v: v.reshape(NCH, 1, 128)
    kern = functools.partial(_conf_kernel, TM=TM, C=C, K=K, HALO=HALO)
    return pl.pallas_call(
        kern,
        grid=(batch, nt),
        in_specs=[pl.BlockSpec((TM, C), lambda b, i: (b * nt + i, 0)),
                  pl.BlockSpec((NCH, K, 128), lambda b, i: (0, 0, 0)),
                  pl.BlockSpec((NCH, 1, 128), lambda b, i: (0, 0, 0)),
                  pl.BlockSpec((NCH, 1, 128), lambda b, i: (0, 0, 0)),
                  pl.BlockSpec((NCH, 1, 128), lambda b, i: (0, 0, 0))],
        out_specs=pl.BlockSpec((TM, C), lambda b, i: (b * nt + i, 0)),
        out_shape=jax.ShapeDtypeStruct((T, C), BF16),
        scratch_shapes=[pltpu.VMEM((NCH, HALO + TM, 128), F32),
                        pltpu.VMEM((NCH, TM, 128), F32)],
        compiler_params=_cparams(("parallel", "arbitrary")),
        name="conformer",
    )(u, w3, chunked(dw_b), chunked(0.5 * ln_g), chunked(0.5 * ln_b))


def _pack_halves(v):
    d2 = v.shape[1] // 2
    hi = lax.bitcast_convert_type(v[:, :d2].astype(BF16).astype(F32), jnp.uint32)
    lo = lax.bitcast_convert_type(v[:, d2:].astype(BF16).astype(F32), jnp.uint32)
    return hi | (lo >> 16)


def _unpack_halves(p):
    hi = lax.bitcast_convert_type(p & jnp.uint32(0xFFFF0000), F32)
    lo = lax.bitcast_convert_type(p << 16, F32)
    return hi, lo


def _router_kernel(x_ref, g_ref, w2_ref, whi_ref, b_ref, h_ref, idx_ref, cw_ref, cnt_ref, *, TM, NG, EPG):
    i = pl.program_id(0)

    @pl.when(i == 0)
    def _():
        cnt_ref[...] = jnp.zeros_like(cnt_ref)

    x = x_ref[...]
    ms = jnp.mean(x * x, axis=-1, keepdims=True)
    h = x * lax.rsqrt(ms + RMS_EPS) * g_ref[...]
    h_ref[...] = _pack_halves(h)

    h_hi = h.astype(BF16)
    h_lo = (h - h_hi.astype(F32)).astype(BF16)
    p2 = jnp.dot(h_hi, w2_ref[...], preferred_element_type=F32)
    p1 = jnp.dot(h_lo, whi_ref[...], preferred_element_type=F32)
    L = ROUTER_LANES
    lg = p2[:, 0:L] + p2[:, L:2 * L] + p1 + b_ref[...]

    lane_i = lax.broadcasted_iota(jnp.int32, (TM, L), 1)
    lane = lane_i.astype(F32)
    gl = jnp.where(lane_i < NG, lg, NEG_BIG)
    gmax = jnp.max(gl, axis=-1, keepdims=True)
    gidx = jnp.min(jnp.where(gl == gmax, lane, float(L)), axis=-1, keepdims=True)
    p_top = 1.0 / jnp.sum(jnp.exp(gl - gmax), axis=-1, keepdims=True)

    egrp = jnp.floor((lane - EXPERT_LANE0) * (1.0 / EPG))
    sel = jnp.logical_and(lane_i >= EXPERT_LANE0, egrp == gidx)
    el = jnp.where(sel, lg, NEG_BIG)
    v0 = jnp.max(el, axis=-1, keepdims=True)
    i0 = jnp.min(jnp.where(el == v0, lane, float(L)), axis=-1, keepdims=True)
    el1 = jnp.where(lane == i0, NEG_BIG, el)
    v1 = jnp.max(el1, axis=-1, keepdims=True)
    i1 = jnp.min(jnp.where(el1 == v1, lane, float(L)), axis=-1, keepdims=True)
    t = jnp.exp(v1 - v0)
    w0 = 1.0 / (1.0 + t)
    c0 = p_top * w0
    c1 = p_top * (t * w0)

    oh0 = lane == i0
    oh1 = lane == i1
    oh = jnp.where(jnp.logical_or(oh0, oh1), 1.0, 0.0)
    r = lax.broadcasted_iota(jnp.int32, (TM, TM), 0)
    cidx = lax.broadcasted_iota(jnp.int32, (TM, TM), 1)
    strict = jnp.where(cidx < r, 1.0, 0.0).astype(BF16)
    before = jnp.dot(strict, oh.astype(BF16), preferred_element_type=F32) + cnt_ref[...]
    rank0 = jnp.sum(jnp.where(oh0, before, 0.0), axis=-1, keepdims=True)
    rank1 = jnp.sum(jnp.where(oh1, before, 0.0), axis=-1, keepdims=True)
    cnt_ref[...] = cnt_ref[...] + jnp.sum(oh, axis=0, keepdims=True)

    e0 = i0 - EXPERT_LANE0
    e1 = i1 - EXPERT_LANE0
    idx = jnp.where(lane_i == 0, e0, jnp.where(lane_i == 1, e1,
          jnp.where(lane_i == 2, rank0, jnp.where(lane_i == 3, rank1, 0.0))))
    idx_ref[...] = idx.astype(jnp.int32)
    cw_ref[...] = jnp.where(lane_i == 0, c0, jnp.where(lane_i == 1, c1, 0.0))


def _router(x1, g, r_grp, r_grp_b, r_exp, r_exp_b):
    T, D = x1.shape
    NG = r_grp.shape[1]
    EPG = r_exp.shape[2]
    NE = NG * EPG
    L = ROUTER_LANES
    assert NG <= EXPERT_LANE0 and EXPERT_LANE0 + NE <= L
    w = jnp.zeros((D, L), F32)
    w = w.at[:, 0:NG].set(r_grp).at[:, EXPERT_LANE0:EXPERT_LANE0 + NE].set(r_exp.reshape(D, NE))
    b = jnp.zeros((1, L), F32)
    b = b.at[0, 0:NG].set(r_grp_b).at[0, EXPERT_LANE0:EXPERT_LANE0 + NE].set(r_exp_b.reshape(NE))
    w_hi = w.astype(BF16)
    w_lo = (w - w_hi.astype(F32)).astype(BF16)
    w2 = jnp.concatenate([w_hi, w_lo], axis=1)
    TM = _tile(T, 256, 8)
    kern = functools.partial(_router_kernel, TM=TM, NG=NG, EPG=EPG)
    return pl.pallas_call(
        kern,
        grid=(T // TM,),
        in_specs=[pl.BlockSpec((TM, D), lambda i: (i, 0)),
                  pl.BlockSpec((1, D), lambda i: (0, 0)),
                  pl.BlockSpec((D, 2 * L), lambda i: (0, 0)),
                  pl.BlockSpec((D, L), lambda i: (0, 0)),
                  pl.BlockSpec((1, L), lambda i: (0, 0))],
        out_specs=[pl.BlockSpec((TM, D // 2), lambda i: (i, 0)),
                   pl.BlockSpec((TM, L), lambda i: (i, 0)),
                   pl.BlockSpec((TM, L), lambda i: (i, 0)),
                   pl.BlockSpec((1, L), lambda i: (0, 0))],
        out_shape=[jax.ShapeDtypeStruct((T, D // 2), jnp.uint32),
                   jax.ShapeDtypeStruct((T, L), jnp.int32),
                   jax.ShapeDtypeStruct((T, L), F32),
                   jax.ShapeDtypeStruct((1, L), F32)],
        compiler_params=_cparams(("arbitrary",)),
        name="router",
    )(x1, g.reshape(1, D), w2, w_hi, b)


def _positions_kernel(idx_ref, starts_ref, pos_ref):
    idx = idx_ref[...]
    lane = lax.broadcasted_iota(jnp.int32, idx.shape, 1)
    starts = starts_ref[...]

    def lookup(e):
        return jnp.sum(jnp.where(lane == e + EXPERT_LANE0, starts, 0.0), axis=-1, keepdims=True)

    pos0 = lookup(idx[:, 0:1]).astype(jnp.int32) + idx[:, 2:3]
    pos1 = lookup(idx[:, 1:2]).astype(jnp.int32) + idx[:, 3:4]
    pos_ref[...] = jnp.where(lane == 0, pos0, jnp.where(lane == 1, pos1, 0))


def _positions(idx, starts_row):
    T, L = idx.shape
    TM = _tile(T, 1024, 8)
    return pl.pallas_call(
        _positions_kernel,
        grid=(T // TM,),
        in_specs=[pl.BlockSpec((TM, L), lambda i: (i, 0)),
                  pl.BlockSpec((1, L), lambda i: (0, 0))],
        out_specs=pl.BlockSpec((TM, L), lambda i: (i, 0)),
        out_shape=jax.ShapeDtypeStruct((T, L), jnp.int32),
        compiler_params=_cparams(("parallel",)),
        name="moe_positions",
    )(idx, starts_row)


def _dispatch_kernel(pos0_ref, pos1_ref, h_ref, xs_in_ref, xs_ref, sem, *, TM):
    del xs_in_ref
    base = pl.program_id(0) * TM

    def copies(r):
        src = h_ref.at[pl.ds(r, 1)]
        return (pltpu.make_async_copy(src, xs_ref.at[pl.ds(pos0_ref[base + r], 1)], sem.at[0]),
                pltpu.make_async_copy(src, xs_ref.at[pl.ds(pos1_ref[base + r], 1)], sem.at[1]))

    def start(r, carry):
        for cp in copies(r):
            cp.start()
        return carry

    def wait(r, carry):
        for cp in copies(r):
            cp.wait()
        return carry

    lax.fori_loop(0, TM, start, 0, unroll=DMA_UNROLL)
    lax.fori_loop(0, TM, wait, 0, unroll=DMA_UNROLL)


def _dispatch(h2, pos0, pos1, rows):
    T, D2 = h2.shape
    TM = _tile(T, 256, 8)
    grid_spec = pltpu.PrefetchScalarGridSpec(
        num_scalar_prefetch=2,
        grid=(T // TM,),
        in_specs=[pl.BlockSpec((TM, D2), lambda i, p0, p1: (i, 0)),
                  pl.BlockSpec(memory_space=pl.ANY)],
        out_specs=pl.BlockSpec(memory_space=pl.ANY),
        scratch_shapes=[pltpu.SemaphoreType.DMA((2,))],
    )
    return pl.pallas_call(
        functools.partial(_dispatch_kernel, TM=TM),
        grid_spec=grid_spec,
        out_shape=jax.ShapeDtypeStruct((rows, D2), h2.dtype),
        input_output_aliases={3: 0},
        compiler_params=_cparams(("arbitrary",)),
        name="moe_dispatch",
    )(pos0, pos1, h2, jnp.zeros((rows, D2), h2.dtype))


def _expert_weight_copies(hbm_refs, stage_refs, sem, e, s):
    out = []
    for m, (h, st) in enumerate(zip(hbm_refs, stage_refs)):
        half = h.shape[1] // 2
        for p in range(2):
            rows = pl.ds(p * half, half)
            out.append((pltpu.make_async_copy(h.at[e, rows], st.at[s, rows], sem.at[s, 2 * m + p]), p))
    return out


def _stage_expert_weights(te_ref, first_ref, slot_ref, next_ref, hbm_refs, stage_refs, w16_refs, sem):
    i = pl.program_id(0)

    @pl.when(i == 0)
    def _():
        for cp, prio in _expert_weight_copies(hbm_refs, stage_refs, sem, te_ref[0], 0):
            cp.start(priority=prio)

    @pl.when(first_ref[i] == 1)
    def _():
        s = slot_ref[i]

        @pl.when(next_ref[i] >= 0)
        def _():
            for cp, prio in _expert_weight_copies(hbm_refs, stage_refs, sem, next_ref[i], 1 - s):
                cp.start(priority=prio)

        for cp, _ in _expert_weight_copies(hbm_refs, stage_refs, sem, te_ref[i], s):
            cp.wait()
        for st, w16 in zip(stage_refs, w16_refs):
            w16[...] = st[s].astype(BF16)


def _experts_up_kernel(te_ref, ts_ref, first_ref, slot_ref, next_ref, nu_ref,
                       x_ref, wg_hbm, wu_hbm, hid_ref, wg_st, wu_st, wg16_ref, wu16_ref, sem):
    del ts_ref
    i = pl.program_id(0)
    _stage_expert_weights(te_ref, first_ref, slot_ref, next_ref,
                          (wg_hbm, wu_hbm), (wg_st, wu_st), (wg16_ref, wu16_ref), sem)

    @pl.when(i < nu_ref[0])
    def _():
        hi, lo = _unpack_halves(x_ref[...])
        x = jnp.concatenate([hi.astype(BF16), lo.astype(BF16)], axis=1)
        g = jnp.dot(x, wg16_ref[...], preferred_element_type=F32)
        u = jnp.dot(x, wu16_ref[...], preferred_element_type=F32)
        hid_ref[...] = (_silu(g) * u).astype(hid_ref.dtype)

    @pl.when(i >= nu_ref[0])
    def _():
        hid_ref[...] = jnp.zeros_like(hid_ref)


def _experts_up(xs, w_gate, w_up, plan, tme):
    R, D2 = xs.shape
    NE, D, F = w_gate.shape
    nt = R // tme
    nsp = len(plan)
    grid_spec = pltpu.PrefetchScalarGridSpec(
        num_scalar_prefetch=nsp,
        grid=(nt,),
        in_specs=[pl.BlockSpec((tme, D2), lambda i, te, ts, *_: (ts[i], 0)),
                  pl.BlockSpec(memory_space=pl.ANY),
                  pl.BlockSpec(memory_space=pl.ANY)],
        out_specs=pl.BlockSpec((tme, F), lambda i, *_: (i, 0)),
        scratch_shapes=[pltpu.VMEM((2, D, F), F32), pltpu.VMEM((2, D, F), F32),
                        pltpu.VMEM((D, F), BF16), pltpu.VMEM((D, F), BF16),
                        pltpu.SemaphoreType.DMA((2, 4))],
    )
    return pl.pallas_call(
        _experts_up_kernel,
        grid_spec=grid_spec,
        out_shape=jax.ShapeDtypeStruct((R, F), BF16),
        compiler_params=_cparams(("arbitrary",)),
        name="moe_experts_up",
    )(*plan, xs, w_gate, w_up)


def _experts_down_kernel(te_ref, ts_ref, first_ref, slot_ref, next_ref, nu_ref,
                         hid_ref, wd_hbm, y_ref, wd_st, wd16_ref, sem):
    del ts_ref
    i = pl.program_id(0)
    _stage_expert_weights(te_ref, first_ref, slot_ref, next_ref, (wd_hbm,), (wd_st,), (wd16_ref,), sem)

    @pl.when(i < nu_ref[0])
    def _():
        y = jnp.dot(hid_ref[...], wd16_ref[...], preferred_element_type=F32)
        y_ref[...] = _pack_halves(y)

    @pl.when(i >= nu_ref[0])
    def _():
        y_ref[...] = jnp.zeros_like(y_ref)


def _experts_down(hid, w_down, plan, tme):
    R, F = hid.shape
    NE, _, D = w_down.shape
    nt = R // tme
    grid_spec = pltpu.PrefetchScalarGridSpec(
        num_scalar_prefetch=len(plan),
        grid=(nt,),
        in_specs=[pl.BlockSpec((tme, F), lambda i, *_: (i, 0)),
                  pl.BlockSpec(memory_space=pl.ANY)],
        out_specs=pl.BlockSpec((tme, D // 2), lambda i, *_: (i, 0)),
        scratch_shapes=[pltpu.VMEM((2, F, D), F32), pltpu.VMEM((F, D), BF16),
                        pltpu.SemaphoreType.DMA((2, 2))],
    )
    return pl.pallas_call(
        _experts_down_kernel,
        grid_spec=grid_spec,
        out_shape=jax.ShapeDtypeStruct((R, D // 2), jnp.uint32),
        compiler_params=_cparams(("arbitrary",)),
        name="moe_experts_down",
    )(*plan, hid, w_down)


def _combine_kernel(pos0_ref, pos1_ref, x_ref, cw_ref, g_ref, y_ref, o_ref, buf, sem, *, TM, final_norm):
    i = pl.program_id(0)
    n = pl.num_programs(0)
    D2 = x_ref.shape[1] // 2

    def copies(step, slot, r):
        t = step * TM + r
        return (pltpu.make_async_copy(y_ref.at[pl.ds(pos0_ref[t], 1)], buf.at[slot, 0, pl.ds(r, 1)], sem.at[slot, 0]),
                pltpu.make_async_copy(y_ref.at[pl.ds(pos1_ref[t], 1)], buf.at[slot, 1, pl.ds(r, 1)], sem.at[slot, 1]))

    def start_all(step, slot):
        def body(r, carry):
            for cp in copies(step, slot, r):
                cp.start()
            return carry
        lax.fori_loop(0, TM, body, 0, unroll=DMA_UNROLL)

    def wait_all(step, slot):
        def body(r, carry):
            for cp in copies(step, slot, r):
                cp.wait()
            return carry
        lax.fori_loop(0, TM, body, 0, unroll=DMA_UNROLL)

    slot = lax.rem(i, COMBINE_SLOTS)
    slot_ahead = lax.rem(i + 2, COMBINE_SLOTS)

    @pl.when(i == 0)
    def _():
        start_all(0, 0)

        @pl.when(n > 1)
        def _():
            start_all(1, 1)

    wait_all(i, slot)

    RB = 8

    def rows_body(issue_ahead):
        def body(rb, carry):
            if issue_ahead:
                for r in range(RB):
                    for cp in copies(i + 2, slot_ahead, rb * RB + r):
                        cp.start()
            rows = pl.ds(pl.multiple_of(rb * RB, RB), RB)
            cw = cw_ref[rows, :]
            c0 = cw[:, 0:1]
            c1 = cw[:, 1:2]
            y0h, y0l = _unpack_halves(buf[slot, 0, rows, :])
            y1h, y1l = _unpack_halves(buf[slot, 1, rows, :])
            xh = x_ref[rows, :D2] + c0 * y0h + c1 * y1h
            xl = x_ref[rows, D2:] + c0 * y0l + c1 * y1l
            if final_norm:
                ssq = jnp.sum(xh * xh, axis=-1, keepdims=True) + jnp.sum(xl * xl, axis=-1, keepdims=True)
                rstd = lax.rsqrt(ssq * (1.0 / (2 * D2)) + RMS_EPS)
                xh = xh * rstd * g_ref[:, :D2]
                xl = xl * rstd * g_ref[:, D2:]
            o_ref[rows, :D2] = xh
            o_ref[rows, D2:] = xl
            return carry
        return body

    @pl.when(i + 2 < n)
    def _():
        lax.fori_loop(0, TM // RB, rows_body(True), 0, unroll=4)

    @pl.when(i + 2 >= n)
    def _():
        lax.fori_loop(0, TM // RB, rows_body(False), 0, unroll=4)


def _combine(x1, cw, y, pos0, pos1, g, final_norm):
    T, D = x1.shape
    TM = _tile(T, 256, 8)
    L = cw.shape[1]
    grid_spec = pltpu.PrefetchScalarGridSpec(
        num_scalar_prefetch=2,
        grid=(T // TM,),
        in_specs=[pl.BlockSpec((TM, D), lambda i, p0, p1: (i, 0)),
                  pl.BlockSpec((TM, L), lambda i, p0, p1: (i, 0)),
                  pl.BlockSpec((1, D), lambda i, p0, p1: (0, 0)),
                  pl.BlockSpec(memory_space=pl.ANY)],
        out_specs=pl.BlockSpec((TM, D), lambda i, p0, p1: (i, 0)),
        scratch_shapes=[pltpu.VMEM((COMBINE_SLOTS, 2, TM, D // 2), jnp.uint32),
                        pltpu.SemaphoreType.DMA((COMBINE_SLOTS, 2))],
    )
    return pl.pallas_call(
        functools.partial(_combine_kernel, TM=TM, final_norm=final_norm),
        grid_spec=grid_spec,
        out_shape=jax.ShapeDtypeStruct((T, D), F32),
        compiler_params=_cparams(("arbitrary",)),
        name="moe_combine",
    )(pos0, pos1, x1, cw, g.reshape(1, D), y)


def _moe(x1, norm_g, r_grp, r_grp_b, r_exp, r_exp_b, w_gate, w_up, w_down, final_g, final_norm):
    T, D = x1.shape
    NG, EPG = r_exp.shape[1], r_exp.shape[2]
    NE = NG * EPG
    F = w_gate.shape[-1]
    tme = 256
    h2, idx, cw, cnt = _router(x1, norm_g, r_grp, r_grp_b, r_exp, r_exp_b)

    counts = cnt[0, EXPERT_LANE0:EXPERT_LANE0 + NE].astype(jnp.int32)
    padded = ((counts + tme - 1) // tme) * tme
    ends = jnp.cumsum(padded)
    starts = ends - padded
    nt = (2 * T + NE * (tme - 1) + tme - 1) // tme
    n_used = (ends[-1] // tme).astype(jnp.int32)
    tile_ids = jnp.minimum(jnp.arange(nt, dtype=jnp.int32), n_used - 1)
    tile_expert = jnp.minimum(
        jnp.sum((ends[None, :] <= (tile_ids * tme)[:, None]).astype(jnp.int32), axis=1), NE - 1)
    starts_row = jnp.zeros((1, ROUTER_LANES), F32).at[0, EXPERT_LANE0:EXPERT_LANE0 + NE].set(starts.astype(F32))
    ids = jnp.arange(NE, dtype=jnp.int32)
    later = jnp.logical_and(ids[None, :] > ids[:, None], (counts > 0)[None, :])
    next_expert = jnp.min(jnp.where(later, ids[None, :], NE), axis=1)
    next_expert = jnp.where(next_expert == NE, -1, next_expert).astype(jnp.int32)
    prev_expert = jnp.concatenate([jnp.full((1,), -1, jnp.int32), tile_expert[:-1]])
    first = jnp.logical_and(tile_expert != prev_expert, jnp.arange(nt) < n_used).astype(jnp.int32)
    slot = ((jnp.cumsum(first) - 1) % 2).astype(jnp.int32)
    plan = (tile_expert, tile_ids, first, slot, next_expert[tile_expert], n_used.reshape(1))

    pos = _positions(idx, starts_row)
    pos0, pos1 = pos[:, 0], pos[:, 1]
    xs = _dispatch(h2, pos0, pos1, nt * tme)
    hid = _experts_up(xs, w_gate.reshape(NE, D, F), w_up.reshape(NE, D, F), plan, tme)
    y = _experts_down(hid, w_down.reshape(NE, F, D), plan, tme)
    return _combine(x1, cw, y, pos0, pos1, final_g, final_norm)


def kernel(x, norm_mix, w_in, ssd_conv_w, ssd_conv_b, ssd_dt_bias, ssd_a_log, ssd_d, ssd_norm, ssd_w_out, conf_glu_b, conf_dw_w, conf_dw_b, conf_ln_g, conf_ln_b, conf_w_out, w_out, norm_ffn, router_group, router_group_b, router_expert, router_expert_b, expert_w_gate, expert_w_up, expert_w_down, norm_final):
    B, S, D = x.shape
    T = B * S
    depth = w_in.shape[0]
    d_ssd = ssd_norm.shape[1]
    d_xbc = ssd_conv_w.shape[2]
    H = ssd_a_log.shape[1]
    d_conf = conf_dw_w.shape[2]
    o_xbc = d_ssd
    o_dt = o_xbc + d_xbc
    o_glu = o_dt + H
    o_gate = o_glu + 2 * d_conf

    xf = x.reshape(T, D)
    for l in range(depth):
        wl = w_in[l]
        h, dt, dtT = _norm_proj_dt(xf, norm_mix[l], wl, o_dt, H, ssd_dt_bias[l])
        zs = _proj_act(h, wl, 0, d_ssd, "silu", BF16, "proj_z")
        xbc = _proj_act(h, wl, o_xbc, d_xbc, "none", BF16, "proj_xbc")
        glu = _proj_glu(h, wl, o_glu, o_glu + d_conf, d_conf,
                        conf_glu_b[l, :d_conf], conf_glu_b[l, d_conf:], BF16)
        gates = _proj_act(h, wl, o_gate, 2 * D, "sigmoid", BF16, "proj_gates")

        yn = _ssd(xbc, zs, dt, dtT, ssd_conv_w[l], ssd_conv_b[l], ssd_a_log[l], ssd_d[l], ssd_norm[l],
                  batch=B, seq=S, d_ssd=d_ssd)
        uc = _conformer(glu, conf_dw_w[l], conf_dw_b[l], conf_ln_g[l], conf_ln_b[l], batch=B, seq=S)

        m1 = _out_gate(yn, ssd_w_out[l].astype(BF16), gates, 0, None, F32, "out_ssd")
        mixed = _out_gate(uc, conf_w_out[l].astype(BF16), gates, D, m1, BF16, "out_conf")
        x1 = _out_res(mixed, w_out[l].astype(BF16), xf)

        xf = _moe(x1, norm_ffn[l], router_group[l], router_group_b[l], router_expert[l],
                  router_expert_b[l], expert_w_gate[l], expert_w_up[l], expert_w_down[l],
                  norm_final, l == depth - 1)
    return xf.reshape(B, S, D)
```

```python
import functools

import jax
import jax.numpy as jnp
from jax import lax
from jax.experimental import pallas as pl
from jax.experimental.pallas import tpu as pltpu

SSD_GROUPS = 8
SSD_CHUNK = 128
SSD_CHUNKS_PER_STEP = 2
RMS_EPS = 1e-6
LN_EPS = 1e-5
NEG_BIG = -1e30
LOG2E = 1.4426950408889634
LANES = 128
DMA_UNROLL = 8
COMBINE_SLOTS = 3
ROUTER_LANES = 128
EXPERT_LANE0 = 64
VMEM_LIMIT_BYTES = 56 * 1024 * 1024

F32 = jnp.float32
BF16 = jnp.bfloat16


def _cparams(semantics):
    return pltpu.CompilerParams(dimension_semantics=semantics,
                                vmem_limit_bytes=VMEM_LIMIT_BYTES)


def _tile(n, pref, mult=128):
    if n <= pref:
        return n
    t = (pref // mult) * mult
    while t >= mult:
        if n % t == 0:
            return t
        t -= mult
    return n


def _sigmoid(x):
    return 0.5 + 0.5 * jnp.tanh(0.5 * x)


def _silu(x):
    h = 0.5 * x
    return h + h * jnp.tanh(h)


def _softplus(x):
    return jnp.maximum(x, 0.0) + jnp.log(1.0 + jnp.exp(-jnp.abs(x)))


def _wcols(K, tn, col0):
    assert col0 % LANES == 0 and tn % LANES == 0
    return pl.BlockSpec((pl.Element(K), pl.Element(tn)),
                        lambda j, i: (0, (col0 // LANES + j * (tn // LANES)) * LANES))


def _proj_act_kernel(a_ref, w_ref, o_ref, w16_ref, *, act):
    @pl.when(pl.program_id(1) == 0)
    def _():
        w16_ref[...] = w_ref[...].astype(BF16)

    acc = jnp.dot(a_ref[...], w16_ref[...], preferred_element_type=F32)
    if act == "silu":
        acc = _silu(acc)
    elif act == "sigmoid":
        acc = _sigmoid(acc)
    o_ref[...] = acc.astype(o_ref.dtype)


def _proj_act(a, w, col0, n, act, out_dtype, name):
    M, K = a.shape
    tm, tn = _tile(M, 512, 8), _tile(n, 1024)
    return pl.pallas_call(
        functools.partial(_proj_act_kernel, act=act),
        grid=(n // tn, M // tm),
        in_specs=[pl.BlockSpec((tm, K), lambda j, i: (i, 0)),
                  _wcols(K, tn, col0)],
        out_specs=pl.BlockSpec((tm, tn), lambda j, i: (i, j)),
        out_shape=jax.ShapeDtypeStruct((M, n), out_dtype),
        scratch_shapes=[pltpu.VMEM((K, tn), BF16)],
        compiler_params=_cparams(("parallel", "arbitrary")),
        name=name,
    )(a, w)


def _proj_glu_kernel(a_ref, wu_ref, wg_ref, bu_ref, bg_ref, o_ref, wu16_ref, wg16_ref):
    @pl.when(pl.program_id(1) == 0)
    def _():
        wu16_ref[...] = wu_ref[...].astype(BF16)
        wg16_ref[...] = wg_ref[...].astype(BF16)

    a = a_ref[...]
    u = jnp.dot(a, wu16_ref[...], preferred_element_type=F32) + bu_ref[...]
    g = jnp.dot(a, wg16_ref[...], preferred_element_type=F32) + bg_ref[...]
    o_ref[...] = (u * _sigmoid(g)).astype(o_ref.dtype)


def _proj_glu(a, w, col_u, col_g, n, bu, bg, out_dtype):
    M, K = a.shape
    tm, tn = _tile(M, 1024, 8), _tile(n, 256)
    return pl.pallas_call(
        _proj_glu_kernel,
        grid=(n // tn, M // tm),
        in_specs=[pl.BlockSpec((tm, K), lambda j, i: (i, 0)),
                  _wcols(K, tn, col_u),
                  _wcols(K, tn, col_g),
                  pl.BlockSpec((1, tn), lambda j, i: (0, j)),
                  pl.BlockSpec((1, tn), lambda j, i: (0, j))],
        out_specs=pl.BlockSpec((tm, tn), lambda j, i: (i, j)),
        out_shape=jax.ShapeDtypeStruct((M, n), out_dtype),
        scratch_shapes=[pltpu.VMEM((K, tn), BF16), pltpu.VMEM((K, tn), BF16)],
        compiler_params=_cparams(("parallel", "arbitrary")),
        name="proj_glu",
    )(a, w, w, bu.reshape(1, n), bg.reshape(1, n))


def _norm_proj_dt_kernel(x_ref, g_ref, w_ref, b_ref, h_ref, dt_ref, dtT_ref):
    x = x_ref[...]
    ms = jnp.mean(x * x, axis=-1, keepdims=True)
    h = (x * lax.rsqrt(ms + RMS_EPS) * g_ref[...]).astype(BF16)
    h_ref[...] = h
    acc = jnp.dot(h, w_ref[...].astype(BF16), preferred_element_type=F32) + b_ref[...]
    dt = _softplus(acc)
    dt_ref[...] = dt
    dtT_ref[...] = dt.T


def _norm_proj_dt(x, g, w, col0, H, b):
    M, K = x.shape
    tm = _tile(M, 256, 128)
    return pl.pallas_call(
        _norm_proj_dt_kernel,
        grid=(M // tm,),
        in_specs=[pl.BlockSpec((tm, K), lambda i: (i, 0)),
                  pl.BlockSpec((1, K), lambda i: (0, 0)),
                  pl.BlockSpec((pl.Element(K), pl.Element(H)), lambda i: (0, col0)),
                  pl.BlockSpec((1, H), lambda i: (0, 0))],
        out_specs=[pl.BlockSpec((tm, K), lambda i: (i, 0)),
                   pl.BlockSpec((tm, H), lambda i: (i, 0)),
                   pl.BlockSpec((H, tm), lambda i: (0, i))],
        out_shape=[jax.ShapeDtypeStruct((M, K), BF16),
                   jax.ShapeDtypeStruct((M, H), F32),
                   jax.ShapeDtypeStruct((H, M), F32)],
        compiler_params=_cparams(("parallel",)),
        name="norm_proj_dt",
    )(x, g.reshape(1, K), w, b.reshape(1, H))


def _out_gate_kernel(*refs, has_prev):
    if has_prev:
        a_ref, w_ref, g_ref, p_ref, o_ref = refs
    else:
        a_ref, w_ref, g_ref, o_ref = refs
    acc = jnp.dot(a_ref[...], w_ref[...], preferred_element_type=F32)
    acc = acc * g_ref[...].astype(F32)
    if has_prev:
        acc = acc + p_ref[...].astype(F32)
    o_ref[...] = acc.astype(o_ref.dtype)


def _out_gate(a, w, gates, gate_col0, prev, out_dtype, name):
    M, K = a.shape
    N = w.shape[1]
    tm, tn = _tile(M, 512, 8), _tile(N, (4 * 1024 * 1024) // K)
    goff = gate_col0 // tn
    in_specs = [pl.BlockSpec((tm, K), lambda j, i: (i, 0)),
                pl.BlockSpec((K, tn), lambda j, i: (0, j)),
                pl.BlockSpec((tm, tn), lambda j, i: (i, j + goff))]
    args = [a, w, gates]
    if prev is not None:
        in_specs.append(pl.BlockSpec((tm, tn), lambda j, i: (i, j)))
        args.append(prev)
    return pl.pallas_call(
        functools.partial(_out_gate_kernel, has_prev=prev is not None),
        grid=(N // tn, M // tm),
        in_specs=in_specs,
        out_specs=pl.BlockSpec((tm, tn), lambda j, i: (i, j)),
        out_shape=jax.ShapeDtypeStruct((M, N), out_dtype),
        compiler_params=_cparams(("parallel", "parallel")),
        name=name,
    )(*args)


def _out_res_kernel(a_ref, w_ref, r_ref, o_ref):
    acc = jnp.dot(a_ref[...], w_ref[...], preferred_element_type=F32)
    o_ref[...] = r_ref[...] + acc


def _out_res(a, w, res):
    M, K = a.shape
    N = w.shape[1]
    tm, tn = _tile(M, 512, 8), _tile(N, 1024)
    return pl.pallas_call(
        _out_res_kernel,
        grid=(N // tn, M // tm),
        in_specs=[pl.BlockSpec((tm, K), lambda j, i: (i, 0)),
                  pl.BlockSpec((K, tn), lambda j, i: (0, j)),
                  pl.BlockSpec((tm, tn), lambda j, i: (i, j))],
        out_specs=pl.BlockSpec((tm, tn), lambda j, i: (i, j)),
        out_shape=jax.ShapeDtypeStruct((M, N), F32),
        compiler_params=_cparams(("parallel", "parallel")),
        name="out_res",
    )(a, w, res)


def _split3(v):
    hi = v.astype(BF16)
    r1 = v - hi.astype(F32)
    mid = r1.astype(BF16)
    lo = (r1 - mid.astype(F32)).astype(BF16)
    return hi, mid, lo


def _ssd_kernel(xs_ref, b_ref, c_ref, wx_ref, wb_ref, wc_ref, bx_ref, bb_ref, bc_ref,
                dt_ref, dtT_ref, alog_ref, alogT_ref, z_ref, dskip_ref, ng_ref,
                o_ref, ext_ref, state_ref, y_ref, *, Q, Hg, P, N, K):
    W = Hg * P
    HALO = 8
    NPL = Hg // 2
    c = pl.program_id(2)

    @pl.when(c == 0)
    def _():
        ext_ref[:, 0:HALO, :] = jnp.zeros((NPL + 2, HALO, 2 * P), F32)
        state_ref[...] = jnp.zeros_like(state_ref)

    for r0 in range(0, xs_ref.shape[0], Q):
        _ssd_chunk(xs_ref, b_ref, c_ref, wx_ref, wb_ref, wc_ref, bx_ref, bb_ref, bc_ref,
                   dt_ref, dtT_ref, alog_ref, alogT_ref, z_ref, dskip_ref, ng_ref,
                   o_ref, ext_ref, state_ref, y_ref, r0, Q=Q, Hg=Hg, P=P, N=N, K=K)


def _ssd_chunk(xs_ref, b_ref, c_ref, wx_ref, wb_ref, wc_ref, bx_ref, bb_ref, bc_ref,
               dt_ref, dtT_ref, alog_ref, alogT_ref, z_ref, dskip_ref, ng_ref,
               o_ref, ext_ref, state_ref, y_ref, r0, *, Q, Hg, P, N, K):
    W = Hg * P
    HALO = 8
    NPL = Hg // 2
    rows = slice(r0, r0 + Q)

    for p in range(NPL):
        ext_ref[p, HALO:HALO + Q, :] = xs_ref[rows, p * 2 * P:(p + 1) * 2 * P].astype(F32)
    ext_ref[NPL, HALO:HALO + Q, :] = b_ref[rows, :].astype(F32)
    ext_ref[NPL + 1, HALO:HALO + Q, :] = c_ref[rows, :].astype(F32)

    def conv_silu(plane, w_ref, bias_ref, wlo):
        acc = bias_ref[:, wlo:wlo + 2 * P]
        for k in range(K):
            r0 = HALO - (K - 1) + k
            acc = acc + w_ref[k:k + 1, wlo:wlo + 2 * P] * ext_ref[plane, r0:r0 + Q, :]
        return acc + acc * jnp.tanh(acc)

    bm = conv_silu(NPL, wb_ref, bb_ref, 0)
    cm = conv_silu(NPL + 1, wc_ref, bc_ref, 0)

    dt = dt_ref[rows, :]
    dtT = dtT_ref[:, rows]
    a = dt * (-jnp.exp(alog_ref[...]) * LOG2E)
    aT = dtT * (-jnp.exp(alogT_ref[...]) * LOG2E)

    row = lax.broadcasted_iota(jnp.int32, (Q, Q), 0)
    col = lax.broadcasted_iota(jnp.int32, (Q, Q), 1)
    causal = col <= row
    tril = jnp.where(causal, 1.0, 0.0).astype(BF16)
    triu = jnp.where(row <= col, 1.0, 0.0).astype(BF16)
    acum = jnp.dot(jnp.concatenate([tril, tril, tril], axis=1),
                   jnp.concatenate(_split3(a), axis=0), preferred_element_type=F32)
    acumT = jnp.dot(jnp.concatenate(_split3(aT), axis=1),
                    jnp.concatenate([triu, triu, triu], axis=0), preferred_element_type=F32)

    bm16 = bm.astype(BF16)
    cm16 = cm.astype(BF16)
    cb = lax.dot_general(cm16, bm16, (((1,), (1,)), ((), ())), preferred_element_type=F32)
    bT = bm.T

    lane = lax.broadcasted_iota(jnp.int32, (1, 2 * P), 1)
    lo_half = lane < P
    zero16 = jnp.zeros((), BF16)

    ssq = jnp.zeros((Q, 1), F32)
    for q in range(Hg // 2):
        l0 = q * 2 * P
        x_pair = conv_silu(q, wx_ref, bx_ref, l0)
        x16 = x_pair.astype(BF16)
        rhs_x = jnp.concatenate([jnp.where(lo_half, x16, zero16),
                                 jnp.where(lo_half, zero16, x16)], axis=0)
        s_old = state_ref[q]
        s16 = s_old.astype(BF16)
        rhs_s = jnp.concatenate([jnp.where(lo_half, s16, zero16),
                                 jnp.where(lo_half, zero16, s16)], axis=0)
        l_parts, ec_parts, bw_parts, g_parts = [], [], [], []
        for hh in range(2):
            h = 2 * q + hh
            ai = jnp.broadcast_to(acum[:, h:h + 1], (Q, Q))
            aj = acumT[h:h + 1, :]
            dtj = dtT[h:h + 1, :]
            seg = jnp.where(causal, ai - aj, NEG_BIG)
            l_parts.append((jnp.exp2(seg) * cb * dtj).astype(BF16))
            ec_parts.append((jnp.exp2(ai) * cm).astype(BF16))
            a_last = acumT[h:h + 1, Q - 1:Q]
            w_end = jnp.exp2(a_last - aj) * dtj
            bw_parts.append((bT * w_end).astype(BF16))
            g_parts.append(jnp.exp2(a_last))
        lhs_y = jnp.concatenate(l_parts + ec_parts, axis=1)
        rhs_y = jnp.concatenate([rhs_x, rhs_s], axis=0)
        y = jnp.dot(lhs_y, rhs_y, preferred_element_type=F32)
        upd = jnp.dot(jnp.concatenate(bw_parts, axis=1), rhs_x,
                      preferred_element_type=F32)
        decay = jnp.where(lo_half, g_parts[0], g_parts[1])
        state_ref[q] = s_old * decay + upd

        y = y + dskip_ref[:, l0:l0 + 2 * P] * x_pair
        y = y * z_ref[rows, l0:l0 + 2 * P].astype(F32)
        y_ref[:, l0:l0 + 2 * P] = y
        ssq = ssq + jnp.sum(y * y, axis=-1, keepdims=True)

    ext_ref[:, 0:HALO, :] = ext_ref[:, Q:Q + HALO, :]
    rstd = lax.rsqrt(ssq * (1.0 / W) + RMS_EPS)
    o_ref[rows, :] = (y_ref[...] * rstd * ng_ref[...]).astype(o_ref.dtype)


def _ssd(xbc, zs, dt, dtT, conv_w, conv_b, a_log, d_skip, norm_g, *, batch, seq, d_ssd):
    T = batch * seq
    G = SSD_GROUPS
    Q = SSD_CHUNK
    H = a_log.shape[0]
    Hg = H // G
    P = d_ssd // H
    W = Hg * P
    N = (xbc.shape[1] - d_ssd) // (2 * G)
    K = conv_w.shape[0]
    assert 2 * P == 128 and N == 128 and Q == 128 and Hg % 2 == 0 and W % 128 == 0
    assert seq % Q == 0 and K - 1 <= 8
    QB = Q * SSD_CHUNKS_PER_STEP if seq % (Q * SSD_CHUNKS_PER_STEP) == 0 else Q
    nc = seq // QB
    nb_x = d_ssd // N

    dt_g = dt.reshape(T, G, Hg).transpose(1, 0, 2)
    alog_g = a_log.reshape(G, 1, Hg)
    alogT = a_log.reshape(H, 1)
    dskip = jnp.repeat(d_skip, P).reshape(1, d_ssd)
    conv_w = 0.5 * conv_w
    cb2 = (0.5 * conv_b).reshape(1, -1)

    rowblk = lambda b, g, c: b * nc + c
    kern = functools.partial(_ssd_kernel, Q=Q, Hg=Hg, P=P, N=N, K=K)
    return pl.pallas_call(
        kern,
        grid=(batch, G, nc),
        in_specs=[
            pl.BlockSpec((QB, W), lambda b, g, c: (rowblk(b, g, c), g)),
            pl.BlockSpec((QB, N), lambda b, g, c: (rowblk(b, g, c), nb_x + g)),
            pl.BlockSpec((QB, N), lambda b, g, c: (rowblk(b, g, c), nb_x + G + g)),
            pl.BlockSpec((K, W), lambda b, g, c: (0, g)),
            pl.BlockSpec((K, N), lambda b, g, c: (0, nb_x + g)),
            pl.BlockSpec((K, N), lambda b, g, c: (0, nb_x + G + g)),
            pl.BlockSpec((1, W), lambda b, g, c: (0, g)),
            pl.BlockSpec((1, N), lambda b, g, c: (0, nb_x + g)),
            pl.BlockSpec((1, N), lambda b, g, c: (0, nb_x + G + g)),
            pl.BlockSpec((None, QB, Hg), lambda b, g, c: (g, rowblk(b, g, c), 0)),
            pl.BlockSpec((Hg, QB), lambda b, g, c: (g, rowblk(b, g, c))),
            pl.BlockSpec((None, 1, Hg), lambda b, g, c: (g, 0, 0)),
            pl.BlockSpec((Hg, 1), lambda b, g, c: (g, 0)),
            pl.BlockSpec((QB, W), lambda b, g, c: (rowblk(b, g, c), g)),
            pl.BlockSpec((1, W), lambda b, g, c: (0, g)),
            pl.BlockSpec((1, W), lambda b, g, c: (0, g)),
        ],
        out_specs=pl.BlockSpec((QB, W), lambda b, g, c: (rowblk(b, g, c), g)),
        out_shape=jax.ShapeDtypeStruct((T, d_ssd), BF16),
        scratch_shapes=[pltpu.VMEM((Hg // 2 + 2, Q + 8, 2 * P), F32),
                        pltpu.VMEM((Hg // 2, N, 2 * P), F32),
                        pltpu.VMEM((Q, W), F32)],
        compiler_params=_cparams(("parallel", "parallel", "arbitrary")),
        name="ssd_scan",
    )(xbc, xbc, xbc, conv_w, conv_w, conv_w, cb2, cb2, cb2,
      dt_g, dtT, alog_g, alogT, zs, dskip, norm_g.reshape(1, d_ssd))


def _conf_kernel(u_ref, w_ref, b_ref, g_ref, beta_ref, o_ref, ext_ref, conv_ref, *, TM, C, K, HALO):
    NCH = C // 128
    i = pl.program_id(1)

    @pl.when(i == 0)
    def _():
        ext_ref[:, 0:HALO, :] = jnp.zeros((NCH, HALO, 128), F32)

    for ch in range(NCH):
        ext_ref[ch, HALO:HALO + TM, :] = u_ref[:, ch * 128:(ch + 1) * 128].astype(F32)

    def chunk_body(ch, carry):
        s1, = carry
        acc = jnp.broadcast_to(b_ref[ch], (TM, 128))
        for k in range(K):
            r0 = HALO - (K - 1) + k
            acc = acc + w_ref[ch, k:k + 1, :] * ext_ref[ch, r0:r0 + TM, :]
        conv_ref[ch] = acc
        ext_ref[ch, 0:HALO, :] = ext_ref[ch, TM:TM + HALO, :]
        return (s1 + acc,)

    s1, = lax.fori_loop(0, NCH, chunk_body, (jnp.zeros((TM, 128), F32),))
    mu = jnp.sum(s1, axis=-1, keepdims=True) * (1.0 / C)

    def var_body(ch, s2):
        d = conv_ref[ch] - mu
        return s2 + d * d

    s2 = lax.fori_loop(0, NCH, var_body, jnp.zeros((TM, 128), F32))
    rstd = lax.rsqrt(jnp.sum(s2, axis=-1, keepdims=True) * (1.0 / C) + LN_EPS)
    for ch in range(NCH):
        h = (conv_ref[ch] - mu) * rstd * g_ref[ch] + beta_ref[ch]
        o_ref[:, ch * 128:(ch + 1) * 128] = (h + h * jnp.tanh(h)).astype(o_ref.dtype)


def _conformer(u, dw_w, dw_b, ln_g, ln_b, *, batch, seq):
    T, C = u.shape
    K = dw_w.shape[0]
    HALO = 32
    assert K - 1 <= HALO and C % 128 == 0
    TM = _tile(seq, 256, 32)
    nt = seq // TM
    NCH = C // 128
    w3 = dw_w.reshape(K, NCH, 128).transpose(1, 0, 2)
    chunked = lambda v: v.reshape(NCH, 1, 128)
    kern = functools.partial(_conf_kernel, TM=TM, C=C, K=K, HALO=HALO)
    return pl.pallas_call(
        kern,
        grid=(batch, nt),
        in_specs=[pl.BlockSpec((TM, C), lambda b, i: (b * nt + i, 0)),
                  pl.BlockSpec((NCH, K, 128), lambda b, i: (0, 0, 0)),
                  pl.BlockSpec((NCH, 1, 128), lambda b, i: (0, 0, 0)),
                  pl.BlockSpec((NCH, 1, 128), lambda b, i: (0, 0, 0)),
                  pl.BlockSpec((NCH, 1, 128), lambda b, i: (0, 0, 0))],
        out_specs=pl.BlockSpec((TM, C), lambda b, i: (b * nt + i, 0)),
        out_shape=jax.ShapeDtypeStruct((T, C), BF16),
        scratch_shapes=[pltpu.VMEM((NCH, HALO + TM, 128), F32),
                        pltpu.VMEM((NCH, TM, 128), F32)],
        compiler_params=_cparams(("parallel", "arbitrary")),
        name="conformer",
    )(u, w3, chunked(dw_b), chunked(0.5 * ln_g), chunked(0.5 * ln_b))


def _pack_halves(v):
    d2 = v.shape[1] // 2
    hi = lax.bitcast_convert_type(v[:, :d2].astype(BF16).astype(F32), jnp.uint32)
    lo = lax.bitcast_convert_type(v[:, d2:].astype(BF16).astype(F32), jnp.uint32)
    return hi | (lo >> 16)


def _unpack_halves(p):
    hi = lax.bitcast_convert_type(p & jnp.uint32(0xFFFF0000), F32)
    lo = lax.bitcast_convert_type(p << 16, F32)
    return hi, lo


def _router_kernel(x_ref, g_ref, w2_ref, whi_ref, b_ref, h_ref, idx_ref, cw_ref, cnt_ref, *, TM, NG, EPG):
    i = pl.program_id(0)

    @pl.when(i == 0)
    def _():
        cnt_ref[...] = jnp.zeros_like(cnt_ref)

    x = x_ref[...]
    ms = jnp.mean(x * x, axis=-1, keepdims=True)
    h = x * lax.rsqrt(ms + RMS_EPS) * g_ref[...]
    h_ref[...] = _pack_halves(h)

    h_hi = h.astype(BF16)
    h_lo = (h - h_hi.astype(F32)).astype(BF16)
    p2 = jnp.dot(h_hi, w2_ref[...], preferred_element_type=F32)
    p1 = jnp.dot(h_lo, whi_ref[...], preferred_element_type=F32)
    L = ROUTER_LANES
    lg = p2[:, 0:L] + p2[:, L:2 * L] + p1 + b_ref[...]

    lane_i = lax.broadcasted_iota(jnp.int32, (TM, L), 1)
    lane = lane_i.astype(F32)
    gl = jnp.where(lane_i < NG, lg, NEG_BIG)
    gmax = jnp.max(gl, axis=-1, keepdims=True)
    gidx = jnp.min(jnp.where(gl == gmax, lane, float(L)), axis=-1, keepdims=True)
    p_top = 1.0 / jnp.sum(jnp.exp(gl - gmax), axis=-1, keepdims=True)

    egrp = jnp.floor((lane - EXPERT_LANE0) * (1.0 / EPG))
    sel = jnp.logical_and(lane_i >= EXPERT_LANE0, egrp == gidx)
    el = jnp.where(sel, lg, NEG_BIG)
    v0 = jnp.max(el, axis=-1, keepdims=True)
    i0 = jnp.min(jnp.where(el == v0, lane, float(L)), axis=-1, keepdims=True)
    el1 = jnp.where(lane == i0, NEG_BIG, el)
    v1 = jnp.max(el1, axis=-1, keepdims=True)
    i1 = jnp.min(jnp.where(el1 == v1, lane, float(L)), axis=-1, keepdims=True)
    t = jnp.exp(v1 - v0)
    w0 = 1.0 / (1.0 + t)
    c0 = p_top * w0
    c1 = p_top * (t * w0)

    oh0 = lane == i0
    oh1 = lane == i1
    oh = jnp.where(jnp.logical_or(oh0, oh1), 1.0, 0.0)
    r = lax.broadcasted_iota(jnp.int32, (TM, TM), 0)
    cidx = lax.broadcasted_iota(jnp.int32, (TM, TM), 1)
    strict = jnp.where(cidx < r, 1.0, 0.0).astype(BF16)
    before = jnp.dot(strict, oh.astype(BF16), preferred_element_type=F32) + cnt_ref[...]
    rank0 = jnp.sum(jnp.where(oh0, before, 0.0), axis=-1, keepdims=True)
    rank1 = jnp.sum(jnp.where(oh1, before, 0.0), axis=-1, keepdims=True)
    cnt_ref[...] = cnt_ref[...] + jnp.sum(oh, axis=0, keepdims=True)

    e0 = i0 - EXPERT_LANE0
    e1 = i1 - EXPERT_LANE0
    idx = jnp.where(lane_i == 0, e0, jnp.where(lane_i == 1, e1,
          jnp.where(lane_i == 2, rank0, jnp.where(lane_i == 3, rank1, 0.0))))
    idx_ref[...] = idx.astype(jnp.int32)
    cw_ref[...] = jnp.where(lane_i == 0, c0, jnp.where(lane_i == 1, c1, 0.0))


def _router(x1, g, r_grp, r_grp_b, r_exp, r_exp_b):
    T, D = x1.shape
    NG = r_grp.shape[1]
    EPG = r_exp.shape[2]
    NE = NG * EPG
    L = ROUTER_LANES
    assert NG <= EXPERT_LANE0 and EXPERT_LANE0 + NE <= L
    w = jnp.zeros((D, L), F32)
    w = w.at[:, 0:NG].set(r_grp).at[:, EXPERT_LANE0:EXPERT_LANE0 + NE].set(r_exp.reshape(D, NE))
    b = jnp.zeros((1, L), F32)
    b = b.at[0, 0:NG].set(r_grp_b).at[0, EXPERT_LANE0:EXPERT_LANE0 + NE].set(r_exp_b.reshape(NE))
    w_hi = w.astype(BF16)
    w_lo = (w - w_hi.astype(F32)).astype(BF16)
    w2 = jnp.concatenate([w_hi, w_lo], axis=1)
    TM = _tile(T, 256, 8)
    kern = functools.partial(_router_kernel, TM=TM, NG=NG, EPG=EPG)
    return pl.pallas_call(
        kern,
        grid=(T // TM,),
        in_specs=[pl.BlockSpec((TM, D), lambda i: (i, 0)),
                  pl.BlockSpec((1, D), lambda i: (0, 0)),
                  pl.BlockSpec((D, 2 * L), lambda i: (0, 0)),
                  pl.BlockSpec((D, L), lambda i: (0, 0)),
                  pl.BlockSpec((1, L), lambda i: (0, 0))],
        out_specs=[pl.BlockSpec((TM, D // 2), lambda i: (i, 0)),
                   pl.BlockSpec((TM, L), lambda i: (i, 0)),
                   pl.BlockSpec((TM, L), lambda i: (i, 0)),
                   pl.BlockSpec((1, L), lambda i: (0, 0))],
        out_shape=[jax.ShapeDtypeStruct((T, D // 2), jnp.uint32),
                   jax.ShapeDtypeStruct((T, L), jnp.int32),
                   jax.ShapeDtypeStruct((T, L), F32),
                   jax.ShapeDtypeStruct((1, L), F32)],
        compiler_params=_cparams(("arbitrary",)),
        name="router",
    )(x1, g.reshape(1, D), w2, w_hi, b)


def _positions_kernel(idx_ref, starts_ref, pos_ref):
    idx = idx_ref[...]
    lane = lax.broadcasted_iota(jnp.int32, idx.shape, 1)
    starts = starts_ref[...]

    def lookup(e):
        return jnp.sum(jnp.where(lane == e + EXPERT_LANE0, starts, 0.0), axis=-1, keepdims=True)

    pos0 = lookup(idx[:, 0:1]).astype(jnp.int32) + idx[:, 2:3]
    pos1 = lookup(idx[:, 1:2]).astype(jnp.int32) + idx[:, 3:4]
    pos_ref[...] = jnp.where(lane == 0, pos0, jnp.where(lane == 1, pos1, 0))


def _positions(idx, starts_row):
    T, L = idx.shape
    TM = _tile(T, 1024, 8)
    return pl.pallas_call(
        _positions_kernel,
        grid=(T // TM,),
        in_specs=[pl.BlockSpec((TM, L), lambda i: (i, 0)),
                  pl.BlockSpec((1, L), lambda i: (0, 0))],
        out_specs=pl.BlockSpec((TM, L), lambda i: (i, 0)),
        out_shape=jax.ShapeDtypeStruct((T, L), jnp.int32),
        compiler_params=_cparams(("parallel",)),
        name="moe_positions",
    )(idx, starts_row)


def _dispatch_kernel(pos0_ref, pos1_ref, h_ref, xs_in_ref, xs_ref, sem, *, TM):
    del xs_in_ref
    base = pl.program_id(0) * TM

    def copies(r):
        src = h_ref.at[pl.ds(r, 1)]
        return (pltpu.make_async_copy(src, xs_ref.at[pl.ds(pos0_ref[base + r], 1)], sem.at[0]),
                pltpu.make_async_copy(src, xs_ref.at[pl.ds(pos1_ref[base + r], 1)], sem.at[1]))

    def start(r, carry):
        for cp in copies(r):
            cp.start()
        return carry

    def wait(r, carry):
        for cp in copies(r):
            cp.wait()
        return carry

    lax.fori_loop(0, TM, start, 0, unroll=DMA_UNROLL)
    lax.fori_loop(0, TM, wait, 0, unroll=DMA_UNROLL)


def _dispatch(h2, pos0, pos1, rows):
    T, D2 = h2.shape
    TM = _tile(T, 256, 8)
    grid_spec = pltpu.PrefetchScalarGridSpec(
        num_scalar_prefetch=2,
        grid=(T // TM,),
        in_specs=[pl.BlockSpec((TM, D2), lambda i, p0, p1: (i, 0)),
                  pl.BlockSpec(memory_space=pl.ANY)],
        out_specs=pl.BlockSpec(memory_space=pl.ANY),
        scratch_shapes=[pltpu.SemaphoreType.DMA((2,))],
    )
    return pl.pallas_call(
        functools.partial(_dispatch_kernel, TM=TM),
        grid_spec=grid_spec,
        out_shape=jax.ShapeDtypeStruct((rows, D2), h2.dtype),
        input_output_aliases={3: 0},
        compiler_params=_cparams(("arbitrary",)),
        name="moe_dispatch",
    )(pos0, pos1, h2, jnp.zeros((rows, D2), h2.dtype))


def _expert_weight_copies(hbm_refs, stage_refs, sem, e, s):
    out = []
    for m, (h, st) in enumerate(zip(hbm_refs, stage_refs)):
        half = h.shape[1] // 2
        for p in range(2):
            rows = pl.ds(p * half, half)
            out.append((pltpu.make_async_copy(h.at[e, rows], st.at[s, rows], sem.at[s, 2 * m + p]), p))
    return out


def _stage_expert_weights(te_ref, first_ref, slot_ref, next_ref, hbm_refs, stage_refs, w16_refs, sem):
    i = pl.program_id(0)

    @pl.when(i == 0)
    def _():
        for cp, prio in _expert_weight_copies(hbm_refs, stage_refs, sem, te_ref[0], 0):
            cp.start(priority=prio)

    @pl.when(first_ref[i] == 1)
    def _():
        s = slot_ref[i]

        @pl.when(next_ref[i] >= 0)
        def _():
            for cp, prio in _expert_weight_copies(hbm_refs, stage_refs, sem, next_ref[i], 1 - s):
                cp.start(priority=prio)

        for cp, _ in _expert_weight_copies(hbm_refs, stage_refs, sem, te_ref[i], s):
            cp.wait()
        for st, w16 in zip(stage_refs, w16_refs):
            w16[...] = st[s].astype(BF16)


def _experts_up_kernel(te_ref, ts_ref, first_ref, slot_ref, next_ref, nu_ref,
                       x_ref, wg_hbm, wu_hbm, hid_ref, wg_st, wu_st, wg16_ref, wu16_ref, sem):
    del ts_ref
    i = pl.program_id(0)
    _stage_expert_weights(te_ref, first_ref, slot_ref, next_ref,
                          (wg_hbm, wu_hbm), (wg_st, wu_st), (wg16_ref, wu16_ref), sem)

    @pl.when(i < nu_ref[0])
    def _():
        hi, lo = _unpack_halves(x_ref[...])
        x = jnp.concatenate([hi.astype(BF16), lo.astype(BF16)], axis=1)
        g = jnp.dot(x, wg16_ref[...], preferred_element_type=F32)
        u = jnp.dot(x, wu16_ref[...], preferred_element_type=F32)
        hid_ref[...] = (_silu(g) * u).astype(hid_ref.dtype)

    @pl.when(i >= nu_ref[0])
    def _():
        hid_ref[...] = jnp.zeros_like(hid_ref)


def _experts_up(xs, w_gate, w_up, plan, tme):
    R, D2 = xs.shape
    NE, D, F = w_gate.shape
    nt = R // tme
    nsp = len(plan)
    grid_spec = pltpu.PrefetchScalarGridSpec(
        num_scalar_prefetch=nsp,
        grid=(nt,),
        in_specs=[pl.BlockSpec((tme, D2), lambda i, te, ts, *_: (ts[i], 0)),
                  pl.BlockSpec(memory_space=pl.ANY),
                  pl.BlockSpec(memory_space=pl.ANY)],
        out_specs=pl.BlockSpec((tme, F), lambda i, *_: (i, 0)),
        scratch_shapes=[pltpu.VMEM((2, D, F), F32), pltpu.VMEM((2, D, F), F32),
                        pltpu.VMEM((D, F), BF16), pltpu.VMEM((D, F), BF16),
                        pltpu.SemaphoreType.DMA((2, 4))],
    )
    return pl.pallas_call(
        _experts_up_kernel,
        grid_spec=grid_spec,
        out_shape=jax.ShapeDtypeStruct((R, F), BF16),
        compiler_params=_cparams(("arbitrary",)),
        name="moe_experts_up",
    )(*plan, xs, w_gate, w_up)


def _experts_down_kernel(te_ref, ts_ref, first_ref, slot_ref, next_ref, nu_ref,
                         hid_ref, wd_hbm, y_ref, wd_st, wd16_ref, sem):
    del ts_ref
    i = pl.program_id(0)
    _stage_expert_weights(te_ref, first_ref, slot_ref, next_ref, (wd_hbm,), (wd_st,), (wd16_ref,), sem)

    @pl.when(i < nu_ref[0])
    def _():
        y = jnp.dot(hid_ref[...], wd16_ref[...], preferred_element_type=F32)
        y_ref[...] = _pack_halves(y)

    @pl.when(i >= nu_ref[0])
    def _():
        y_ref[...] = jnp.zeros_like(y_ref)


def _experts_down(hid, w_down, plan, tme):
    R, F = hid.shape
    NE, _, D = w_down.shape
    nt = R // tme
    grid_spec = pltpu.PrefetchScalarGridSpec(
        num_scalar_prefetch=len(plan),
        grid=(nt,),
        in_specs=[pl.BlockSpec((tme, F), lambda i, *_: (i, 0)),
                  pl.BlockSpec(memory_space=pl.ANY)],
        out_specs=pl.BlockSpec((tme, D // 2), lambda i, *_: (i, 0)),
        scratch_shapes=[pltpu.VMEM((2, F, D), F32), pltpu.VMEM((F, D), BF16),
                        pltpu.SemaphoreType.DMA((2, 2))],
    )
    return pl.pallas_call(
        _experts_down_kernel,
        grid_spec=grid_spec,
        out_shape=jax.ShapeDtypeStruct((R, D // 2), jnp.uint32),
        compiler_params=_cparams(("arbitrary",)),
        name="moe_experts_down",
    )(*plan, hid, w_down)


def _combine_kernel(pos0_ref, pos1_ref, x_ref, cw_ref, g_ref, y_ref, o_ref, buf, sem, *, TM, final_norm):
    i = pl.program_id(0)
    n = pl.num_programs(0)
    D2 = x_ref.shape[1] // 2

    def copies(step, slot, r):
        t = step * TM + r
        return (pltpu.make_async_copy(y_ref.at[pl.ds(pos0_ref[t], 1)], buf.at[slot, 0, pl.ds(r, 1)], sem.at[slot, 0]),
                pltpu.make_async_copy(y_ref.at[pl.ds(pos1_ref[t], 1)], buf.at[slot, 1, pl.ds(r, 1)], sem.at[slot, 1]))

    def start_all(step, slot):
        def body(r, carry):
            for cp in copies(step, slot, r):
                cp.start()
            return carry
        lax.fori_loop(0, TM, body, 0, unroll=DMA_UNROLL)

    def wait_all(step, slot):
        def body(r, carry):
            for cp in copies(step, slot, r):
                cp.wait()
            return carry
        lax.fori_loop(0, TM, body, 0, unroll=DMA_UNROLL)

    slot = lax.rem(i, COMBINE_SLOTS)
    slot_ahead = lax.rem(i + 2, COMBINE_SLOTS)

    @pl.when(i == 0)
    def _():
        start_all(0, 0)

        @pl.when(n > 1)
        def _():
            start_all(1, 1)

    wait_all(i, slot)

    RB = 8

    def rows_body(issue_ahead):
        def body(rb, carry):
            if issue_ahead:
                for r in range(RB):
                    for cp in copies(i + 2, slot_ahead, rb * RB + r):
                        cp.start()
            rows = pl.ds(pl.multiple_of(rb * RB, RB), RB)
            cw = cw_ref[rows, :]
            c0 = cw[:, 0:1]
            c1 = cw[:, 1:2]
            y0h, y0l = _unpack_halves(buf[slot, 0, rows, :])
            y1h, y1l = _unpack_halves(buf[slot, 1, rows, :])
            xh = x_ref[rows, :D2] + c0 * y0h + c1 * y1h
            xl = x_ref[rows, D2:] + c0 * y0l + c1 * y1l
            if final_norm:
                ssq = jnp.sum(xh * xh, axis=-1, keepdims=True) + jnp.sum(xl * xl, axis=-1, keepdims=True)
                rstd = lax.rsqrt(ssq * (1.0 / (2 * D2)) + RMS_EPS)
                xh = xh * rstd * g_ref[:, :D2]
                xl = xl * rstd * g_ref[:, D2:]
            o_ref[rows, :D2] = xh
            o_ref[rows, D2:] = xl
            return carry
        return body

    @pl.when(i + 2 < n)
    def _():
        lax.fori_loop(0, TM // RB, rows_body(True), 0, unroll=4)

    @pl.when(i + 2 >= n)
    def _():
        lax.fori_loop(0, TM // RB, rows_body(False), 0, unroll=4)


def _combine(x1, cw, y, pos0, pos1, g, final_norm):
    T, D = x1.shape
    TM = _tile(T, 256, 8)
    L = cw.shape[1]
    grid_spec = pltpu.PrefetchScalarGridSpec(
        num_scalar_prefetch=2,
        grid=(T // TM,),
        in_specs=[pl.BlockSpec((TM, D), lambda i, p0, p1: (i, 0)),
                  pl.BlockSpec((TM, L), lambda i, p0, p1: (i, 0)),
                  pl.BlockSpec((1, D), lambda i, p0, p1: (0, 0)),
                  pl.BlockSpec(memory_space=pl.ANY)],
        out_specs=pl.BlockSpec((TM, D), lambda i, p0, p1: (i, 0)),
        scratch_shapes=[pltpu.VMEM((COMBINE_SLOTS, 2, TM, D // 2), jnp.uint32),
                        pltpu.SemaphoreType.DMA((COMBINE_SLOTS, 2))],
    )
    return pl.pallas_call(
        functools.partial(_combine_kernel, TM=TM, final_norm=final_norm),
        grid_spec=grid_spec,
        out_shape=jax.ShapeDtypeStruct((T, D), F32),
        compiler_params=_cparams(("arbitrary",)),
        name="moe_combine",
    )(pos0, pos1, x1, cw, g.reshape(1, D), y)


def _moe(x1, norm_g, r_grp, r_grp_b, r_exp, r_exp_b, w_gate, w_up, w_down, final_g, final_norm):
    T, D = x1.shape
    NG, EPG = r_exp.shape[1], r_exp.shape[2]
    NE = NG * EPG
    F = w_gate.shape[-1]
    tme = 256
    h2, idx, cw, cnt = _router(x1, norm_g, r_grp, r_grp_b, r_exp, r_exp_b)

    counts = cnt[0, EXPERT_LANE0:EXPERT_LANE0 + NE].astype(jnp.int32)
    padded = ((counts + tme - 1) // tme) * tme
    ends = jnp.cumsum(padded)
    starts = ends - padded
    nt = (2 * T + NE * (tme - 1) + tme - 1) // tme
    n_used = (ends[-1] // tme).astype(jnp.int32)
    tile_ids = jnp.minimum(jnp.arange(nt, dtype=jnp.int32), n_used - 1)
    tile_expert = jnp.minimum(
        jnp.sum((ends[None, :] <= (tile_ids * tme)[:, None]).astype(jnp.int32), axis=1), NE - 1)
    starts_row = jnp.zeros((1, ROUTER_LANES), F32).at[0, EXPERT_LANE0:EXPERT_LANE0 + NE].set(starts.astype(F32))
    ids = jnp.arange(NE, dtype=jnp.int32)
    later = jnp.logical_and(ids[None, :] > ids[:, None], (counts > 0)[None, :])
    next_expert = jnp.min(jnp.where(later, ids[None, :], NE), axis=1)
    next_expert = jnp.where(next_expert == NE, -1, next_expert).astype(jnp.int32)
    prev_expert = jnp.concatenate([jnp.full((1,), -1, jnp.int32), tile_expert[:-1]])
    first = jnp.logical_and(tile_expert != prev_expert, jnp.arange(nt) < n_used).astype(jnp.int32)
    slot = ((jnp.cumsum(first) - 1) % 2).astype(jnp.int32)
    plan = (tile_expert, tile_ids, first, slot, next_expert[tile_expert], n_used.reshape(1))

    pos = _positions(idx, starts_row)
    pos0, pos1 = pos[:, 0], pos[:, 1]
    xs = _dispatch(h2, pos0, pos1, nt * tme)
    hid = _experts_up(xs, w_gate.reshape(NE, D, F), w_up.reshape(NE, D, F), plan, tme)
    y = _experts_down(hid, w_down.reshape(NE, F, D), plan, tme)
    return _combine(x1, cw, y, pos0, pos1, final_g, final_norm)


def kernel(x, norm_mix, w_in, ssd_conv_w, ssd_conv_b, ssd_dt_bias, ssd_a_log, ssd_d, ssd_norm, ssd_w_out, conf_glu_b, conf_dw_w, conf_dw_b, conf_ln_g, conf_ln_b, conf_w_out, w_out, norm_ffn, router_group, router_group_b, router_expert, router_expert_b, expert_w_gate, expert_w_up, expert_w_down, norm_final):
    B, S, D = x.shape
    T = B * S
    depth = w_in.shape[0]
    d_ssd = ssd_norm.shape[1]
    d_xbc = ssd_conv_w.shape[2]
    H = ssd_a_log.shape[1]
    d_conf = conf_dw_w.shape[2]
    o_xbc = d_ssd
    o_dt = o_xbc + d_xbc
    o_glu = o_dt + H
    o_gate = o_glu + 2 * d_conf

    xf = x.reshape(T, D)
    for l in range(depth):
        wl = w_in[l]
        h, dt, dtT = _norm_proj_dt(xf, norm_mix[l], wl, o_dt, H, ssd_dt_bias[l])
        zs = _proj_act(h, wl, 0, d_ssd, "silu", BF16, "proj_z")
        xbc = _proj_act(h, wl, o_xbc, d_xbc, "none", BF16, "proj_xbc")
        glu = _proj_glu(h, wl, o_glu, o_glu + d_conf, d_conf,
                        conf_glu_b[l, :d_conf], conf_glu_b[l, d_conf:], BF16)
        gates = _proj_act(h, wl, o_gate, 2 * D, "sigmoid", BF16, "proj_gates")

        yn = _ssd(xbc, zs, dt, dtT, ssd_conv_w[l], ssd_conv_b[l], ssd_a_log[l], ssd_d[l], ssd_norm[l],
                  batch=B, seq=S, d_ssd=d_ssd)
        uc = _conformer(glu, conf_dw_w[l], conf_dw_b[l], conf_ln_g[l], conf_ln_b[l], batch=B, seq=S)

        m1 = _out_gate(yn, ssd_w_out[l].astype(BF16), gates, 0, None, F32, "out_ssd")
        mixed = _out_gate(uc, conf_w_out[l].astype(BF16), gates, D, m1, BF16, "out_conf")
        x1 = _out_res(mixed, w_out[l].astype(BF16), xf)

        xf = _moe(x1, norm_ffn[l], router_group[l], router_group_b[l], router_expert[l],
                  router_expert_b[l], expert_w_gate[l], expert_w_up[l], expert_w_down[l],
                  norm_final, l == depth - 1)
    return xf.reshape(B, S, D)
```

```python
import functools

import jax
import jax.numpy as jnp
from jax import lax
from jax.experimental import pallas as pl
from jax.experimental.pallas import tpu as pltpu

SSD_GROUPS = 8
SSD_CHUNK = 128
SSD_CHUNKS_PER_STEP = 2
RMS_EPS = 1e-6
LN_EPS = 1e-5
NEG_BIG = -1e30
LOG2E = 1.4426950408889634
LANES = 128
DMA_UNROLL = 8
COMBINE_SLOTS = 3
ROUTER_LANES = 128
EXPERT_LANE0 = 64
VMEM_LIMIT_BYTES = 56 * 1024 * 1024

F32 = jnp.float32
BF16 = jnp.bfloat16


def _cparams(semantics):
    return pltpu.CompilerParams(dimension_semantics=semantics,
                                vmem_limit_bytes=VMEM_LIMIT_BYTES)


def _tile(n, pref, mult=128):
    if n <= pref:
        return n
    t = (pref // mult) * mult
    while t >= mult:
        if n % t == 0:
            return t
        t -= mult
    return n


def _sigmoid(x):
    return 0.5 + 0.5 * jnp.tanh(0.5 * x)


def _silu(x):
    h = 0.5 * x
    return h + h * jnp.tanh(h)


def _softplus(x):
    return jnp.maximum(x, 0.0) + jnp.log(1.0 + jnp.exp(-jnp.abs(x)))


def _wcols(K, tn, col0):
    assert col0 % LANES == 0 and tn % LANES == 0
    return pl.BlockSpec((pl.Element(K), pl.Element(tn)),
                        lambda j, i: (0, (col0 // LANES + j * (tn // LANES)) * LANES))


def _proj_act_kernel(a_ref, w_ref, o_ref, w16_ref, *, act):
    @pl.when(pl.program_id(1) == 0)
    def _():
        w16_ref[...] = w_ref[...].astype(BF16)

    acc = jnp.dot(a_ref[...], w16_ref[...], preferred_element_type=F32)
    if act == "silu":
        acc = _silu(acc)
    elif act == "sigmoid":
        acc = _sigmoid(acc)
    o_ref[...] = acc.astype(o_ref.dtype)


def _proj_act(a, w, col0, n, act, out_dtype, name):
    M, K = a.shape
    tm, tn = _tile(M, 512, 8), _tile(n, 1024)
    return pl.pallas_call(
        functools.partial(_proj_act_kernel, act=act),
        grid=(n // tn, M // tm),
        in_specs=[pl.BlockSpec((tm, K), lambda j, i: (i, 0)),
                  _wcols(K, tn, col0)],
        out_specs=pl.BlockSpec((tm, tn), lambda j, i: (i, j)),
        out_shape=jax.ShapeDtypeStruct((M, n), out_dtype),
        scratch_shapes=[pltpu.VMEM((K, tn), BF16)],
        compiler_params=_cparams(("parallel", "arbitrary")),
        name=name,
    )(a, w)


def _proj_glu_kernel(a_ref, wu_ref, wg_ref, bu_ref, bg_ref, o_ref, wu16_ref, wg16_ref):
    @pl.when(pl.program_id(1) == 0)
    def _():
        wu16_ref[...] = wu_ref[...].astype(BF16)
        wg16_ref[...] = wg_ref[...].astype(BF16)

    a = a_ref[...]
    u = jnp.dot(a, wu16_ref[...], preferred_element_type=F32) + bu_ref[...]
    g = jnp.dot(a, wg16_ref[...], preferred_element_type=F32) + bg_ref[...]
    o_ref[...] = (u * _sigmoid(g)).astype(o_ref.dtype)


def _proj_glu(a, w, col_u, col_g, n, bu, bg, out_dtype):
    M, K = a.shape
    tm, tn = _tile(M, 1024, 8), _tile(n, 256)
    return pl.pallas_call(
        _proj_glu_kernel,
        grid=(n // tn, M // tm),
        in_specs=[pl.BlockSpec((tm, K), lambda j, i: (i, 0)),
                  _wcols(K, tn, col_u),
                  _wcols(K, tn, col_g),
                  pl.BlockSpec((1, tn), lambda j, i: (0, j)),
                  pl.BlockSpec((1, tn), lambda j, i: (0, j))],
        out_specs=pl.BlockSpec((tm, tn), lambda j, i: (i, j)),
        out_shape=jax.ShapeDtypeStruct((M, n), out_dtype),
        scratch_shapes=[pltpu.VMEM((K, tn), BF16), pltpu.VMEM((K, tn), BF16)],
        compiler_params=_cparams(("parallel", "arbitrary")),
        name="proj_glu",
    )(a, w, w, bu.reshape(1, n), bg.reshape(1, n))


def _norm_proj_dt_kernel(x_ref, g_ref, w_ref, b_ref, h_ref, dt_ref, dtT_ref):
    x = x_ref[...]
    ms = jnp.mean(x * x, axis=-1, keepdims=True)
    h = (x * lax.rsqrt(ms + RMS_EPS) * g_ref[...]).astype(BF16)
    h_ref[...] = h
    acc = jnp.dot(h, w_ref[...].astype(BF16), preferred_element_type=F32) + b_ref[...]
    dt = _softplus(acc)
    dt_ref[...] = dt
    dtT_ref[...] = dt.T


def _norm_proj_dt(x, g, w, col0, H, b):
    M, K = x.shape
    tm = _tile(M, 256, 128)
    return pl.pallas_call(
        _norm_proj_dt_kernel,
        grid=(M // tm,),
        in_specs=[pl.BlockSpec((tm, K), lambda i: (i, 0)),
                  pl.BlockSpec((1, K), lambda i: (0, 0)),
                  pl.BlockSpec((pl.Element(K), pl.Element(H)), lambda i: (0, col0)),
                  pl.BlockSpec((1, H), lambda i: (0, 0))],
        out_specs=[pl.BlockSpec((tm, K), lambda i: (i, 0)),
                   pl.BlockSpec((tm, H), lambda i: (i, 0)),
                   pl.BlockSpec((H, tm), lambda i: (0, i))],
        out_shape=[jax.ShapeDtypeStruct((M, K), BF16),
                   jax.ShapeDtypeStruct((M, H), F32),
                   jax.ShapeDtypeStruct((H, M), F32)],
        compiler_params=_cparams(("parallel",)),
        name="norm_proj_dt",
    )(x, g.reshape(1, K), w, b.reshape(1, H))


def _out_gate_kernel(*refs, has_prev):
    if has_prev:
        a_ref, w_ref, g_ref, p_ref, o_ref = refs
    else:
        a_ref, w_ref, g_ref, o_ref = refs
    acc = jnp.dot(a_ref[...], w_ref[...], preferred_element_type=F32)
    acc = acc * g_ref[...].astype(F32)
    if has_prev:
        acc = acc + p_ref[...].astype(F32)
    o_ref[...] = acc.astype(o_ref.dtype)


def _out_gate(a, w, gates, gate_col0, prev, out_dtype, name):
    M, K = a.shape
    N = w.shape[1]
    tm, tn = _tile(M, (4 * 1024 * 1024) // K, 8), _tile(N, (4 * 1024 * 1024) // K)
    goff = gate_col0 // tn
    in_specs = [pl.BlockSpec((tm, K), lambda j, i: (i, 0)),
                pl.BlockSpec((K, tn), lambda j, i: (0, j)),
                pl.BlockSpec((tm, tn), lambda j, i: (i, j + goff))]
    args = [a, w, gates]
    if prev is not None:
        in_specs.append(pl.BlockSpec((tm, tn), lambda j, i: (i, j)))
        args.append(prev)
    return pl.pallas_call(
        functools.partial(_out_gate_kernel, has_prev=prev is not None),
        grid=(N // tn, M // tm),
        in_specs=in_specs,
        out_specs=pl.BlockSpec((tm, tn), lambda j, i: (i, j)),
        out_shape=jax.ShapeDtypeStruct((M, N), out_dtype),
        compiler_params=_cparams(("parallel", "parallel")),
        name=name,
    )(*args)


def _out_res_kernel(a_ref, w_ref, r_ref, o_ref):
    acc = jnp.dot(a_ref[...], w_ref[...], preferred_element_type=F32)
    o_ref[...] = r_ref[...] + acc


def _out_res(a, w, res):
    M, K = a.shape
    N = w.shape[1]
    tm, tn = _tile(M, 1024, 8), _tile(N, 1024)
    return pl.pallas_call(
        _out_res_kernel,
        grid=(N // tn, M // tm),
        in_specs=[pl.BlockSpec((tm, K), lambda j, i: (i, 0)),
                  pl.BlockSpec((K, tn), lambda j, i: (0, j)),
                  pl.BlockSpec((tm, tn), lambda j, i: (i, j))],
        out_specs=pl.BlockSpec((tm, tn), lambda j, i: (i, j)),
        out_shape=jax.ShapeDtypeStruct((M, N), F32),
        compiler_params=_cparams(("parallel", "parallel")),
        name="out_res",
    )(a, w, res)


def _split3(v):
    hi = v.astype(BF16)
    r1 = v - hi.astype(F32)
    mid = r1.astype(BF16)
    lo = (r1 - mid.astype(F32)).astype(BF16)
    return hi, mid, lo


def _ssd_kernel(xs_ref, b_ref, c_ref, wx_ref, wb_ref, wc_ref, bx_ref, bb_ref, bc_ref,
                dt_ref, dtT_ref, alog_ref, alogT_ref, z_ref, dskip_ref, ng_ref,
                o_ref, ext_ref, state_ref, y_ref, *, Q, Hg, P, N, K):
    W = Hg * P
    HALO = 8
    NPL = Hg // 2
    c = pl.program_id(2)

    @pl.when(c == 0)
    def _():
        ext_ref[:, 0:HALO, :] = jnp.zeros((NPL + 2, HALO, 2 * P), F32)
        state_ref[...] = jnp.zeros_like(state_ref)

    for r0 in range(0, xs_ref.shape[0], Q):
        _ssd_chunk(xs_ref, b_ref, c_ref, wx_ref, wb_ref, wc_ref, bx_ref, bb_ref, bc_ref,
                   dt_ref, dtT_ref, alog_ref, alogT_ref, z_ref, dskip_ref, ng_ref,
                   o_ref, ext_ref, state_ref, y_ref, r0, Q=Q, Hg=Hg, P=P, N=N, K=K)


def _ssd_chunk(xs_ref, b_ref, c_ref, wx_ref, wb_ref, wc_ref, bx_ref, bb_ref, bc_ref,
               dt_ref, dtT_ref, alog_ref, alogT_ref, z_ref, dskip_ref, ng_ref,
               o_ref, ext_ref, state_ref, y_ref, r0, *, Q, Hg, P, N, K):
    W = Hg * P
    HALO = 8
    NPL = Hg // 2
    rows = slice(r0, r0 + Q)

    for p in range(NPL):
        ext_ref[p, HALO:HALO + Q, :] = xs_ref[rows, p * 2 * P:(p + 1) * 2 * P].astype(F32)
    ext_ref[NPL, HALO:HALO + Q, :] = b_ref[rows, :].astype(F32)
    ext_ref[NPL + 1, HALO:HALO + Q, :] = c_ref[rows, :].astype(F32)

    def conv_silu(plane, w_ref, bias_ref, wlo):
        acc = bias_ref[:, wlo:wlo + 2 * P]
        for k in range(K):
            r0 = HALO - (K - 1) + k
            acc = acc + w_ref[k:k + 1, wlo:wlo + 2 * P] * ext_ref[plane, r0:r0 + Q, :]
        return acc + acc * jnp.tanh(acc)

    bm = conv_silu(NPL, wb_ref, bb_ref, 0)
    cm = conv_silu(NPL + 1, wc_ref, bc_ref, 0)

    dt = dt_ref[rows, :]
    dtT = dtT_ref[:, rows]
    a = dt * (-jnp.exp(alog_ref[...]) * LOG2E)
    aT = dtT * (-jnp.exp(alogT_ref[...]) * LOG2E)

    row = lax.broadcasted_iota(jnp.int32, (Q, Q), 0)
    col = lax.broadcasted_iota(jnp.int32, (Q, Q), 1)
    causal = col <= row
    tril = jnp.where(causal, 1.0, 0.0).astype(BF16)
    triu = jnp.where(row <= col, 1.0, 0.0).astype(BF16)
    acum = jnp.dot(jnp.concatenate([tril, tril, tril], axis=1),
                   jnp.concatenate(_split3(a), axis=0), preferred_element_type=F32)
    acumT = jnp.dot(jnp.concatenate(_split3(aT), axis=1),
                    jnp.concatenate([triu, triu, triu], axis=0), preferred_element_type=F32)

    bm16 = bm.astype(BF16)
    cm16 = cm.astype(BF16)
    cb = lax.dot_general(cm16, bm16, (((1,), (1,)), ((), ())), preferred_element_type=F32)
    bT = bm.T

    lane = lax.broadcasted_iota(jnp.int32, (1, 2 * P), 1)
    lo_half = lane < P
    zero16 = jnp.zeros((), BF16)

    ssq = jnp.zeros((Q, 1), F32)
    for q in range(Hg // 2):
        l0 = q * 2 * P
        x_pair = conv_silu(q, wx_ref, bx_ref, l0)
        x16 = x_pair.astype(BF16)
        rhs_x = jnp.concatenate([jnp.where(lo_half, x16, zero16),
                                 jnp.where(lo_half, zero16, x16)], axis=0)
        s_old = state_ref[q]
        s16 = s_old.astype(BF16)
        rhs_s = jnp.concatenate([jnp.where(lo_half, s16, zero16),
                                 jnp.where(lo_half, zero16, s16)], axis=0)
        l_parts, ec_parts, bw_parts, g_parts = [], [], [], []
        for hh in range(2):
            h = 2 * q + hh
            ai = jnp.broadcast_to(acum[:, h:h + 1], (Q, Q))
            aj = acumT[h:h + 1, :]
            dtj = dtT[h:h + 1, :]
            seg = jnp.where(causal, ai - aj, NEG_BIG)
            l_parts.append((jnp.exp2(seg) * cb * dtj).astype(BF16))
            ec_parts.append((jnp.exp2(ai) * cm).astype(BF16))
            a_last = acumT[h:h + 1, Q - 1:Q]
            w_end = jnp.exp2(a_last - aj) * dtj
            bw_parts.append((bT * w_end).astype(BF16))
            g_parts.append(jnp.exp2(a_last))
        lhs_y = jnp.concatenate(l_parts + ec_parts, axis=1)
        rhs_y = jnp.concatenate([rhs_x, rhs_s], axis=0)
        y = jnp.dot(lhs_y, rhs_y, preferred_element_type=F32)
        upd = jnp.dot(jnp.concatenate(bw_parts, axis=1), rhs_x,
                      preferred_element_type=F32)
        decay = jnp.where(lo_half, g_parts[0], g_parts[1])
        state_ref[q] = s_old * decay + upd

        y = y + dskip_ref[:, l0:l0 + 2 * P] * x_pair
        y = y * z_ref[rows, l0:l0 + 2 * P].astype(F32)
        y_ref[:, l0:l0 + 2 * P] = y
        ssq = ssq + jnp.sum(y * y, axis=-1, keepdims=True)

    ext_ref[:, 0:HALO, :] = ext_ref[:, Q:Q + HALO, :]
    rstd = lax.rsqrt(ssq * (1.0 / W) + RMS_EPS)
    o_ref[rows, :] = (y_ref[...] * rstd * ng_ref[...]).astype(o_ref.dtype)


def _ssd(xbc, zs, dt, dtT, conv_w, conv_b, a_log, d_skip, norm_g, *, batch, seq, d_ssd):
    T = batch * seq
    G = SSD_GROUPS
    Q = SSD_CHUNK
    H = a_log.shape[0]
    Hg = H // G
    P = d_ssd // H
    W = Hg * P
    N = (xbc.shape[1] - d_ssd) // (2 * G)
    K = conv_w.shape[0]
    assert 2 * P == 128 and N == 128 and Q == 128 and Hg % 2 == 0 and W % 128 == 0
    assert seq % Q == 0 and K - 1 <= 8
    QB = Q * SSD_CHUNKS_PER_STEP if seq % (Q * SSD_CHUNKS_PER_STEP) == 0 else Q
    nc = seq // QB
    nb_x = d_ssd // N

    dt_g = dt.reshape(T, G, Hg).transpose(1, 0, 2)
    alog_g = a_log.reshape(G, 1, Hg)
    alogT = a_log.reshape(H, 1)
    dskip = jnp.repeat(d_skip, P).reshape(1, d_ssd)
    conv_w = 0.5 * conv_w
    cb2 = (0.5 * conv_b).reshape(1, -1)

    rowblk = lambda b, g, c: b * nc + c
    kern = functools.partial(_ssd_kernel, Q=Q, Hg=Hg, P=P, N=N, K=K)
    return pl.pallas_call(
        kern,
        grid=(batch, G, nc),
        in_specs=[
            pl.BlockSpec((QB, W), lambda b, g, c: (rowblk(b, g, c), g)),
            pl.BlockSpec((QB, N), lambda b, g, c: (rowblk(b, g, c), nb_x + g)),
            pl.BlockSpec((QB, N), lambda b, g, c: (rowblk(b, g, c), nb_x + G + g)),
            pl.BlockSpec((K, W), lambda b, g, c: (0, g)),
            pl.BlockSpec((K, N), lambda b, g, c: (0, nb_x + g)),
            pl.BlockSpec((K, N), lambda b, g, c: (0, nb_x + G + g)),
            pl.BlockSpec((1, W), lambda b, g, c: (0, g)),
            pl.BlockSpec((1, N), lambda b, g, c: (0, nb_x + g)),
            pl.BlockSpec((1, N), lambda b, g, c: (0, nb_x + G + g)),
            pl.BlockSpec((None, QB, Hg), lambda b, g, c: (g, rowblk(b, g, c), 0)),
            pl.BlockSpec((Hg, QB), lambda b, g, c: (g, rowblk(b, g, c))),
            pl.BlockSpec((None, 1, Hg), lambda b, g, c: (g, 0, 0)),
            pl.BlockSpec((Hg, 1), lambda b, g, c: (g, 0)),
            pl.BlockSpec((QB, W), lambda b, g, c: (rowblk(b, g, c), g)),
            pl.BlockSpec((1, W), lambda b, g, c: (0, g)),
            pl.BlockSpec((1, W), lambda b, g, c: (0, g)),
        ],
        out_specs=pl.BlockSpec((QB, W), lambda b, g, c: (rowblk(b, g, c), g)),
        out_shape=jax.ShapeDtypeStruct((T, d_ssd), BF16),
        scratch_shapes=[pltpu.VMEM((Hg // 2 + 2, Q + 8, 2 * P), F32),
                        pltpu.VMEM((Hg // 2, N, 2 * P), F32),
                        pltpu.VMEM((Q, W), F32)],
        compiler_params=_cparams(("parallel", "parallel", "arbitrary")),
        name="ssd_scan",
    )(xbc, xbc, xbc, conv_w, conv_w, conv_w, cb2, cb2, cb2,
      dt_g, dtT, alog_g, alogT, zs, dskip, norm_g.reshape(1, d_ssd))


def _conf_kernel(u_ref, w_ref, b_ref, g_ref, beta_ref, o_ref, ext_ref, conv_ref, *, TM, C, K, HALO):
    NCH = C // 128
    i = pl.program_id(1)

    @pl.when(i == 0)
    def _():
        ext_ref[:, 0:HALO, :] = jnp.zeros((NCH, HALO, 128), F32)

    for ch in range(NCH):
        ext_ref[ch, HALO:HALO + TM, :] = u_ref[:, ch * 128:(ch + 1) * 128].astype(F32)

    def chunk_body(ch, carry):
        s1, = carry
        acc = jnp.broadcast_to(b_ref[ch], (TM, 128))
        for k in range(K):
            r0 = HALO - (K - 1) + k
            acc = acc + w_ref[ch, k:k + 1, :] * ext_ref[ch, r0:r0 + TM, :]
        conv_ref[ch] = acc
        ext_ref[ch, 0:HALO, :] = ext_ref[ch, TM:TM + HALO, :]
        return (s1 + acc,)

    s1, = lax.fori_loop(0, NCH, chunk_body, (jnp.zeros((TM, 128), F32),))
    mu = jnp.sum(s1, axis=-1, keepdims=True) * (1.0 / C)

    def var_body(ch, s2):
        d = conv_ref[ch] - mu
        return s2 + d * d

    s2 = lax.fori_loop(0, NCH, var_body, jnp.zeros((TM, 128), F32))
    rstd = lax.rsqrt(jnp.sum(s2, axis=-1, keepdims=True) * (1.0 / C) + LN_EPS)
    for ch in range(NCH):
        h = (conv_ref[ch] - mu) * rstd * g_ref[ch] + beta_ref[ch]
        o_ref[:, ch * 128:(ch + 1) * 128] = (h + h * jnp.tanh(h)).astype(o_ref.dtype)


def _conformer(u, dw_w, dw_b, ln_g, ln_b, *, batch, seq):
    T, C = u.shape
    K = dw_w.shape[0]
    HALO = 32
    assert K - 1 <= HALO and C % 128 == 0
    TM = _tile(seq, 256, 32)
    nt = seq // TM
    NCH = C // 128
    w3 = dw_w.reshape(K, NCH, 128).transpose(1, 0, 2)
    chunked = lambda v: v.reshape(NCH, 1, 128)
    kern = functools.partial(_conf_kernel, TM=TM, C=C, K=K, HALO=HALO)
    return pl.pallas_call(
        kern,
        grid=(batch, nt),
        in_specs=[pl.BlockSpec((TM, C), lambda b, i: (b * nt + i, 0)),
                  pl.BlockSpec((NCH, K, 128), lambda b, i: (0, 0, 0)),
                  pl.BlockSpec((NCH, 1, 128), lambda b, i: (0, 0, 0)),
                  pl.BlockSpec((NCH, 1, 128), lambda b, i: (0, 0, 0)),
                  pl.BlockSpec((NCH, 1, 128), lambda b, i: (0, 0, 0))],
        out_specs=pl.BlockSpec((TM, C), lambda b, i: (b * nt + i, 0)),
        out_shape=jax.ShapeDtypeStruct((T, C), BF16),
        scratch_shapes=[pltpu.VMEM((NCH, HALO + TM, 128), F32),
                        pltpu.VMEM((NCH, TM, 128), F32)],
        compiler_params=_cparams(("parallel", "arbitrary")),
        name="conformer",
    )(u, w3, chunked(dw_b), chunked(0.5 * ln_g), chunked(0.5 * ln_b))


def _pack_halves(v):
    d2 = v.shape[1] // 2
    hi = lax.bitcast_convert_type(v[:, :d2].astype(BF16).astype(F32), jnp.uint32)
    lo = lax.bitcast_convert_type(v[:, d2:].astype(BF16).astype(F32), jnp.uint32)
    return hi | (lo >> 16)


def _unpack_halves(p):
    hi = lax.bitcast_convert_type(p & jnp.uint32(0xFFFF0000), F32)
    lo = lax.bitcast_convert_type(p << 16, F32)
    return hi, lo


def _router_kernel(x_ref, g_ref, w2_ref, whi_ref, b_ref, h_ref, idx_ref, cw_ref, cnt_ref, *, TM, NG, EPG):
    i = pl.program_id(0)

    @pl.when(i == 0)
    def _():
        cnt_ref[...] = jnp.zeros_like(cnt_ref)

    x = x_ref[...]
    ms = jnp.mean(x * x, axis=-1, keepdims=True)
    h = x * lax.rsqrt(ms + RMS_EPS) * g_ref[...]
    h_ref[...] = _pack_halves(h)

    h_hi = h.astype(BF16)
    h_lo = (h - h_hi.astype(F32)).astype(BF16)
    p2 = jnp.dot(h_hi, w2_ref[...], preferred_element_type=F32)
    p1 = jnp.dot(h_lo, whi_ref[...], preferred_element_type=F32)
    L = ROUTER_LANES
    lg = p2[:, 0:L] + p2[:, L:2 * L] + p1 + b_ref[...]

    lane_i = lax.broadcasted_iota(jnp.int32, (TM, L), 1)
    lane = lane_i.astype(F32)
    gl = jnp.where(lane_i < NG, lg, NEG_BIG)
    gmax = jnp.max(gl, axis=-1, keepdims=True)
    gidx = jnp.min(jnp.where(gl == gmax, lane, float(L)), axis=-1, keepdims=True)
    p_top = 1.0 / jnp.sum(jnp.exp(gl - gmax), axis=-1, keepdims=True)

    egrp = jnp.floor((lane - EXPERT_LANE0) * (1.0 / EPG))
    sel = jnp.logical_and(lane_i >= EXPERT_LANE0, egrp == gidx)
    el = jnp.where(sel, lg, NEG_BIG)
    v0 = jnp.max(el, axis=-1, keepdims=True)
    i0 = jnp.min(jnp.where(el == v0, lane, float(L)), axis=-1, keepdims=True)
    el1 = jnp.where(lane == i0, NEG_BIG, el)
    v1 = jnp.max(el1, axis=-1, keepdims=True)
    i1 = jnp.min(jnp.where(el1 == v1, lane, float(L)), axis=-1, keepdims=True)
    t = jnp.exp(v1 - v0)
    w0 = 1.0 / (1.0 + t)
    c0 = p_top * w0
    c1 = p_top * (t * w0)

    oh0 = lane == i0
    oh1 = lane == i1
    oh = jnp.where(jnp.logical_or(oh0, oh1), 1.0, 0.0)
    r = lax.broadcasted_iota(jnp.int32, (TM, TM), 0)
    cidx = lax.broadcasted_iota(jnp.int32, (TM, TM), 1)
    strict = jnp.where(cidx < r, 1.0, 0.0).astype(BF16)
    before = jnp.dot(strict, oh.astype(BF16), preferred_element_type=F32) + cnt_ref[...]
    rank0 = jnp.sum(jnp.where(oh0, before, 0.0), axis=-1, keepdims=True)
    rank1 = jnp.sum(jnp.where(oh1, before, 0.0), axis=-1, keepdims=True)
    cnt_ref[...] = cnt_ref[...] + jnp.sum(oh, axis=0, keepdims=True)

    e0 = i0 - EXPERT_LANE0
    e1 = i1 - EXPERT_LANE0
    idx = jnp.where(lane_i == 0, e0, jnp.where(lane_i == 1, e1,
          jnp.where(lane_i == 2, rank0, jnp.where(lane_i == 3, rank1, 0.0))))
    idx_ref[...] = idx.astype(jnp.int32)
    cw_ref[...] = jnp.where(lane_i == 0, c0, jnp.where(lane_i == 1, c1, 0.0))


def _router(x1, g, r_grp, r_grp_b, r_exp, r_exp_b):
    T, D = x1.shape
    NG = r_grp.shape[1]
    EPG = r_exp.shape[2]
    NE = NG * EPG
    L = ROUTER_LANES
    assert NG <= EXPERT_LANE0 and EXPERT_LANE0 + NE <= L
    w = jnp.zeros((D, L), F32)
    w = w.at[:, 0:NG].set(r_grp).at[:, EXPERT_LANE0:EXPERT_LANE0 + NE].set(r_exp.reshape(D, NE))
    b = jnp.zeros((1, L), F32)
    b = b.at[0, 0:NG].set(r_grp_b).at[0, EXPERT_LANE0:EXPERT_LANE0 + NE].set(r_exp_b.reshape(NE))
    w_hi = w.astype(BF16)
    w_lo = (w - w_hi.astype(F32)).astype(BF16)
    w2 = jnp.concatenate([w_hi, w_lo], axis=1)
    TM = _tile(T, 256, 8)
    kern = functools.partial(_router_kernel, TM=TM, NG=NG, EPG=EPG)
    return pl.pallas_call(
        kern,
        grid=(T // TM,),
        in_specs=[pl.BlockSpec((TM, D), lambda i: (i, 0)),
                  pl.BlockSpec((1, D), lambda i: (0, 0)),
                  pl.BlockSpec((D, 2 * L), lambda i: (0, 0)),
                  pl.BlockSpec((D, L), lambda i: (0, 0)),
                  pl.BlockSpec((1, L), lambda i: (0, 0))],
        out_specs=[pl.BlockSpec((TM, D // 2), lambda i: (i, 0)),
                   pl.BlockSpec((TM, L), lambda i: (i, 0)),
                   pl.BlockSpec((TM, L), lambda i: (i, 0)),
                   pl.BlockSpec((1, L), lambda i: (0, 0))],
        out_shape=[jax.ShapeDtypeStruct((T, D // 2), jnp.uint32),
                   jax.ShapeDtypeStruct((T, L), jnp.int32),
                   jax.ShapeDtypeStruct((T, L), F32),
                   jax.ShapeDtypeStruct((1, L), F32)],
        compiler_params=_cparams(("arbitrary",)),
        name="router",
    )(x1, g.reshape(1, D), w2, w_hi, b)


def _positions_kernel(idx_ref, starts_ref, pos_ref):
    idx = idx_ref[...]
    lane = lax.broadcasted_iota(jnp.int32, idx.shape, 1)
    starts = starts_ref[...]

    def lookup(e):
        return jnp.sum(jnp.where(lane == e + EXPERT_LANE0, starts, 0.0), axis=-1, keepdims=True)

    pos0 = lookup(idx[:, 0:1]).astype(jnp.int32) + idx[:, 2:3]
    pos1 = lookup(idx[:, 1:2]).astype(jnp.int32) + idx[:, 3:4]
    pos_ref[...] = jnp.where(lane == 0, pos0, jnp.where(lane == 1, pos1, 0))


def _positions(idx, starts_row):
    T, L = idx.shape
    TM = _tile(T, 1024, 8)
    return pl.pallas_call(
        _positions_kernel,
        grid=(T // TM,),
        in_specs=[pl.BlockSpec((TM, L), lambda i: (i, 0)),
                  pl.BlockSpec((1, L), lambda i: (0, 0))],
        out_specs=pl.BlockSpec((TM, L), lambda i: (i, 0)),
        out_shape=jax.ShapeDtypeStruct((T, L), jnp.int32),
        compiler_params=_cparams(("parallel",)),
        name="moe_positions",
    )(idx, starts_row)


def _dispatch_kernel(pos0_ref, pos1_ref, ptile_ref, nu_ref, h_ref, xs_ref, zbuf, sem, zsem, *, TM, TME, NE, NT):
    base = pl.program_id(0) * TM

    @pl.when(pl.program_id(0) == 0)
    def _():
        zbuf[...] = jnp.zeros_like(zbuf)

        def zero_tile(t):
            return pltpu.make_async_copy(zbuf, xs_ref.at[pl.ds(pl.multiple_of(t * TME, TME), TME)], zsem)

        def expert_tiles(op):
            def body(e, carry):
                @pl.when(ptile_ref[e] >= 0)
                def _():
                    op(zero_tile(ptile_ref[e]))
                return carry
            lax.fori_loop(0, NE, body, 0)

        def tail_tiles(op):
            def body(t, carry):
                op(zero_tile(t))
                return carry
            lax.fori_loop(nu_ref[0], NT, body, 0)

        expert_tiles(lambda cp: cp.start())
        tail_tiles(lambda cp: cp.start())
        expert_tiles(lambda cp: cp.wait())
        tail_tiles(lambda cp: cp.wait())

    def copies(r):
        src = h_ref.at[pl.ds(r, 1)]
        return (pltpu.make_async_copy(src, xs_ref.at[pl.ds(pos0_ref[base + r], 1)], sem.at[0]),
                pltpu.make_async_copy(src, xs_ref.at[pl.ds(pos1_ref[base + r], 1)], sem.at[1]))

    def start(r, carry):
        for cp in copies(r):
            cp.start()
        return carry

    def wait(r, carry):
        for cp in copies(r):
            cp.wait()
        return carry

    lax.fori_loop(0, TM, start, 0, unroll=DMA_UNROLL)
    lax.fori_loop(0, TM, wait, 0, unroll=DMA_UNROLL)


def _dispatch(h2, pos0, pos1, pad_tile, n_used, rows, tme):
    T, D2 = h2.shape
    TM = _tile(T, 256, 8)
    grid_spec = pltpu.PrefetchScalarGridSpec(
        num_scalar_prefetch=4,
        grid=(T // TM,),
        in_specs=[pl.BlockSpec((TM, D2), lambda i, *_: (i, 0))],
        out_specs=pl.BlockSpec(memory_space=pl.ANY),
        scratch_shapes=[pltpu.VMEM((tme, D2), h2.dtype),
                        pltpu.SemaphoreType.DMA((2,)),
                        pltpu.SemaphoreType.DMA(())],
    )
    kern = functools.partial(_dispatch_kernel, TM=TM, TME=tme, NE=pad_tile.shape[0], NT=rows // tme)
    return pl.pallas_call(
        kern,
        grid_spec=grid_spec,
        out_shape=jax.ShapeDtypeStruct((rows, D2), h2.dtype),
        compiler_params=_cparams(("arbitrary",)),
        name="moe_dispatch",
    )(pos0, pos1, pad_tile, n_used, h2)


def _expert_weight_copies(hbm_refs, stage_refs, sem, e, s):
    out = []
    for m, (h, st) in enumerate(zip(hbm_refs, stage_refs)):
        half = h.shape[1] // 2
        for p in range(2):
            rows = pl.ds(p * half, half)
            out.append((pltpu.make_async_copy(h.at[e, rows], st.at[s, rows], sem.at[s, 2 * m + p]), p))
    return out


def _stage_expert_weights(te_ref, first_ref, slot_ref, next_ref, hbm_refs, stage_refs, w16_refs, sem):
    i = pl.program_id(0)

    @pl.when(i == 0)
    def _():
        for cp, prio in _expert_weight_copies(hbm_refs, stage_refs, sem, te_ref[0], 0):
            cp.start(priority=prio)

    @pl.when(first_ref[i] == 1)
    def _():
        s = slot_ref[i]

        @pl.when(next_ref[i] >= 0)
        def _():
            for cp, prio in _expert_weight_copies(hbm_refs, stage_refs, sem, next_ref[i], 1 - s):
                cp.start(priority=prio)

        for cp, _ in _expert_weight_copies(hbm_refs, stage_refs, sem, te_ref[i], s):
            cp.wait()
        for st, w16 in zip(stage_refs, w16_refs):
            w16[...] = st[s].astype(BF16)


def _experts_up_kernel(te_ref, ts_ref, first_ref, slot_ref, next_ref, nu_ref,
                       x_ref, wg_hbm, wu_hbm, hid_ref, wg_st, wu_st, wg16_ref, wu16_ref, sem):
    del ts_ref
    i = pl.program_id(0)
    _stage_expert_weights(te_ref, first_ref, slot_ref, next_ref,
                          (wg_hbm, wu_hbm), (wg_st, wu_st), (wg16_ref, wu16_ref), sem)

    @pl.when(i < nu_ref[0])
    def _():
        hi, lo = _unpack_halves(x_ref[...])
        x = jnp.concatenate([hi.astype(BF16), lo.astype(BF16)], axis=1)
        g = jnp.dot(x, wg16_ref[...], preferred_element_type=F32)
        u = jnp.dot(x, wu16_ref[...], preferred_element_type=F32)
        hid_ref[...] = (_silu(g) * u).astype(hid_ref.dtype)

    @pl.when(i >= nu_ref[0])
    def _():
        hid_ref[...] = jnp.zeros_like(hid_ref)


def _experts_up(xs, w_gate, w_up, plan, tme):
    R, D2 = xs.shape
    NE, D, F = w_gate.shape
    nt = R // tme
    nsp = len(plan)
    grid_spec = pltpu.PrefetchScalarGridSpec(
        num_scalar_prefetch=nsp,
        grid=(nt,),
        in_specs=[pl.BlockSpec((tme, D2), lambda i, te, ts, *_: (ts[i], 0)),
                  pl.BlockSpec(memory_space=pl.ANY),
                  pl.BlockSpec(memory_space=pl.ANY)],
        out_specs=pl.BlockSpec((tme, F), lambda i, *_: (i, 0)),
        scratch_shapes=[pltpu.VMEM((2, D, F), F32), pltpu.VMEM((2, D, F), F32),
                        pltpu.VMEM((D, F), BF16), pltpu.VMEM((D, F), BF16),
                        pltpu.SemaphoreType.DMA((2, 4))],
    )
    return pl.pallas_call(
        _experts_up_kernel,
        grid_spec=grid_spec,
        out_shape=jax.ShapeDtypeStruct((R, F), BF16),
        compiler_params=_cparams(("arbitrary",)),
        name="moe_experts_up",
    )(*plan, xs, w_gate, w_up)


def _experts_down_kernel(te_ref, ts_ref, first_ref, slot_ref, next_ref, nu_ref,
                         hid_ref, wd_hbm, y_ref, wd_st, wd16_ref, sem):
    del ts_ref
    i = pl.program_id(0)
    _stage_expert_weights(te_ref, first_ref, slot_ref, next_ref, (wd_hbm,), (wd_st,), (wd16_ref,), sem)

    @pl.when(i < nu_ref[0])
    def _():
        y = jnp.dot(hid_ref[...], wd16_ref[...], preferred_element_type=F32)
        y_ref[...] = _pack_halves(y)

    @pl.when(i >= nu_ref[0])
    def _():
        y_ref[...] = jnp.zeros_like(y_ref)


def _experts_down(hid, w_down, plan, tme):
    R, F = hid.shape
    NE, _, D = w_down.shape
    nt = R // tme
    grid_spec = pltpu.PrefetchScalarGridSpec(
        num_scalar_prefetch=len(plan),
        grid=(nt,),
        in_specs=[pl.BlockSpec((tme, F), lambda i, *_: (i, 0)),
                  pl.BlockSpec(memory_space=pl.ANY)],
        out_specs=pl.BlockSpec((tme, D // 2), lambda i, *_: (i, 0)),
        scratch_shapes=[pltpu.VMEM((2, F, D), F32), pltpu.VMEM((F, D), BF16),
                        pltpu.SemaphoreType.DMA((2, 2))],
    )
    return pl.pallas_call(
        _experts_down_kernel,
        grid_spec=grid_spec,
        out_shape=jax.ShapeDtypeStruct((R, D // 2), jnp.uint32),
        compiler_params=_cparams(("arbitrary",)),
        name="moe_experts_down",
    )(*plan, hid, w_down)


def _combine_kernel(pos0_ref, pos1_ref, x_ref, cw_ref, g_ref, y_ref, o_ref, buf, sem, *, TM, final_norm):
    i = pl.program_id(0)
    n = pl.num_programs(0)
    D2 = x_ref.shape[1] // 2

    def copies(step, slot, r):
        t = step * TM + r
        return (pltpu.make_async_copy(y_ref.at[pl.ds(pos0_ref[t], 1)], buf.at[slot, 0, pl.ds(r, 1)], sem.at[slot, 0]),
                pltpu.make_async_copy(y_ref.at[pl.ds(pos1_ref[t], 1)], buf.at[slot, 1, pl.ds(r, 1)], sem.at[slot, 1]))

    def start_all(step, slot):
        def body(r, carry):
            for cp in copies(step, slot, r):
                cp.start()
            return carry
        lax.fori_loop(0, TM, body, 0, unroll=DMA_UNROLL)

    def wait_all(step, slot):
        def body(r, carry):
            for cp in copies(step, slot, r):
                cp.wait()
            return carry
        lax.fori_loop(0, TM, body, 0, unroll=DMA_UNROLL)

    slot = lax.rem(i, COMBINE_SLOTS)
    slot_ahead = lax.rem(i + 2, COMBINE_SLOTS)

    @pl.when(i == 0)
    def _():
        start_all(0, 0)

        @pl.when(n > 1)
        def _():
            start_all(1, 1)

    wait_all(i, slot)

    RB = 8

    def rows_body(issue_ahead):
        def body(rb, carry):
            if issue_ahead:
                for r in range(RB):
                    for cp in copies(i + 2, slot_ahead, rb * RB + r):
                        cp.start()
            rows = pl.ds(pl.multiple_of(rb * RB, RB), RB)
            cw = cw_ref[rows, :]
            c0 = cw[:, 0:1]
            c1 = cw[:, 1:2]
            y0h, y0l = _unpack_halves(buf[slot, 0, rows, :])
            y1h, y1l = _unpack_halves(buf[slot, 1, rows, :])
            xh = x_ref[rows, :D2] + c0 * y0h + c1 * y1h
            xl = x_ref[rows, D2:] + c0 * y0l + c1 * y1l
            if final_norm:
                ssq = jnp.sum(xh * xh, axis=-1, keepdims=True) + jnp.sum(xl * xl, axis=-1, keepdims=True)
                rstd = lax.rsqrt(ssq * (1.0 / (2 * D2)) + RMS_EPS)
                xh = xh * rstd * g_ref[:, :D2]
                xl = xl * rstd * g_ref[:, D2:]
            o_ref[rows, :D2] = xh
            o_ref[rows, D2:] = xl
            return carry
        return body

    @pl.when(i + 2 < n)
    def _():
        lax.fori_loop(0, TM // RB, rows_body(True), 0, unroll=4)

    @pl.when(i + 2 >= n)
    def _():
        lax.fori_loop(0, TM // RB, rows_body(False), 0, unroll=4)


def _combine(x1, cw, y, pos0, pos1, g, final_norm):
    T, D = x1.shape
    TM = _tile(T, 256, 8)
    L = cw.shape[1]
    grid_spec = pltpu.PrefetchScalarGridSpec(
        num_scalar_prefetch=2,
        grid=(T // TM,),
        in_specs=[pl.BlockSpec((TM, D), lambda i, p0, p1: (i, 0)),
                  pl.BlockSpec((TM, L), lambda i, p0, p1: (i, 0)),
                  pl.BlockSpec((1, D), lambda i, p0, p1: (0, 0)),
                  pl.BlockSpec(memory_space=pl.ANY)],
        out_specs=pl.BlockSpec((TM, D), lambda i, p0, p1: (i, 0)),
        scratch_shapes=[pltpu.VMEM((COMBINE_SLOTS, 2, TM, D // 2), jnp.uint32),
                        pltpu.SemaphoreType.DMA((COMBINE_SLOTS, 2))],
    )
    return pl.pallas_call(
        functools.partial(_combine_kernel, TM=TM, final_norm=final_norm),
        grid_spec=grid_spec,
        out_shape=jax.ShapeDtypeStruct((T, D), F32),
        compiler_params=_cparams(("arbitrary",)),
        name="moe_combine",
    )(pos0, pos1, x1, cw, g.reshape(1, D), y)


def _moe(x1, norm_g, r_grp, r_grp_b, r_exp, r_exp_b, w_gate, w_up, w_down, final_g, final_norm):
    T, D = x1.shape
    NG, EPG = r_exp.shape[1], r_exp.shape[2]
    NE = NG * EPG
    F = w_gate.shape[-1]
    tme = 256
    h2, idx, cw, cnt = _router(x1, norm_g, r_grp, r_grp_b, r_exp, r_exp_b)

    counts = cnt[0, EXPERT_LANE0:EXPERT_LANE0 + NE].astype(jnp.int32)
    padded = ((counts + tme - 1) // tme) * tme
    ends = jnp.cumsum(padded)
    starts = ends - padded
    nt = (2 * T + NE * (tme - 1) + tme - 1) // tme
    n_used = (ends[-1] // tme).astype(jnp.int32)
    tile_ids = jnp.minimum(jnp.arange(nt, dtype=jnp.int32), n_used - 1)
    tile_expert = jnp.minimum(
        jnp.sum((ends[None, :] <= (tile_ids * tme)[:, None]).astype(jnp.int32), axis=1), NE - 1)
    starts_row = jnp.zeros((1, ROUTER_LANES), F32).at[0, EXPERT_LANE0:EXPERT_LANE0 + NE].set(starts.astype(F32))
    ids = jnp.arange(NE, dtype=jnp.int32)
    later = jnp.logical_and(ids[None, :] > ids[:, None], (counts > 0)[None, :])
    next_expert = jnp.min(jnp.where(later, ids[None, :], NE), axis=1)
    next_expert = jnp.where(next_expert == NE, -1, next_expert).astype(jnp.int32)
    prev_expert = jnp.concatenate([jnp.full((1,), -1, jnp.int32), tile_expert[:-1]])
    first = jnp.logical_and(tile_expert != prev_expert, jnp.arange(nt) < n_used).astype(jnp.int32)
    slot = ((jnp.cumsum(first) - 1) % 2).astype(jnp.int32)
    plan = (tile_expert, tile_ids, first, slot, next_expert[tile_expert], n_used.reshape(1))

    pos = _positions(idx, starts_row)
    pos0, pos1 = pos[:, 0], pos[:, 1]
    pad_tile = jnp.where(padded > 0, ends // tme - 1, -1).astype(jnp.int32)
    xs = _dispatch(h2, pos0, pos1, pad_tile, n_used.reshape(1), nt * tme, tme)
    hid = _experts_up(xs, w_gate.reshape(NE, D, F), w_up.reshape(NE, D, F), plan, tme)
    y = _experts_down(hid, w_down.reshape(NE, F, D), plan, tme)
    return _combine(x1, cw, y, pos0, pos1, final_g, final_norm)


def kernel(x, norm_mix, w_in, ssd_conv_w, ssd_conv_b, ssd_dt_bias, ssd_a_log, ssd_d, ssd_norm, ssd_w_out, conf_glu_b, conf_dw_w, conf_dw_b, conf_ln_g, conf_ln_b, conf_w_out, w_out, norm_ffn, router_group, router_group_b, router_expert, router_expert_b, expert_w_gate, expert_w_up, expert_w_down, norm_final):
    B, S, D = x.shape
    T = B * S
    depth = w_in.shape[0]
    d_ssd = ssd_norm.shape[1]
    d_xbc = ssd_conv_w.shape[2]
    H = ssd_a_log.shape[1]
    d_conf = conf_dw_w.shape[2]
    o_xbc = d_ssd
    o_dt = o_xbc + d_xbc
    o_glu = o_dt + H
    o_gate = o_glu + 2 * d_conf

    xf = x.reshape(T, D)
    for l in range(depth):
        wl = w_in[l]
        h, dt, dtT = _norm_proj_dt(xf, norm_mix[l], wl, o_dt, H, ssd_dt_bias[l])
        zs = _proj_act(h, wl, 0, d_ssd, "silu", BF16, "proj_z")
        xbc = _proj_act(h, wl, o_xbc, d_xbc, "none", BF16, "proj_xbc")
        glu = _proj_glu(h, wl, o_glu, o_glu + d_conf, d_conf,
                        conf_glu_b[l, :d_conf], conf_glu_b[l, d_conf:], BF16)
        gates = _proj_act(h, wl, o_gate, 2 * D, "sigmoid", BF16, "proj_gates")

        yn = _ssd(xbc, zs, dt, dtT, ssd_conv_w[l], ssd_conv_b[l], ssd_a_log[l], ssd_d[l], ssd_norm[l],
                  batch=B, seq=S, d_ssd=d_ssd)
        uc = _conformer(glu, conf_dw_w[l], conf_dw_b[l], conf_ln_g[l], conf_ln_b[l], batch=B, seq=S)

        m1 = _out_gate(yn, ssd_w_out[l].astype(BF16), gates, 0, None, F32, "out_ssd")
        mixed = _out_gate(uc, conf_w_out[l].astype(BF16), gates, D, m1, BF16, "out_conf")
        x1 = _out_res(mixed, w_out[l].astype(BF16), xf)

        xf = _moe(x1, norm_ffn[l], router_group[l], router_group_b[l], router_expert[l],
                  router_expert_b[l], expert_w_gate[l], expert_w_up[l], expert_w_down[l],
                  norm_final, l == depth - 1)
    return xf.reshape(B, S, D)
```

```python
import functools

import jax
import jax.numpy as jnp
from jax import lax
from jax.experimental import pallas as pl
from jax.experimental.pallas import tpu as pltpu

SSD_GROUPS = 8
SSD_CHUNK = 128
SSD_CHUNKS_PER_STEP = 2
RMS_EPS = 1e-6
LN_EPS = 1e-5
NEG_BIG = -1e30
LOG2E = 1.4426950408889634
LANES = 128
DMA_UNROLL = 8
COMBINE_SLOTS = 3
ROUTER_LANES = 128
EXPERT_LANE0 = 64
VMEM_LIMIT_BYTES = 56 * 1024 * 1024

F32 = jnp.float32
BF16 = jnp.bfloat16


def _cparams(semantics):
    return pltpu.CompilerParams(dimension_semantics=semantics,
                                vmem_limit_bytes=VMEM_LIMIT_BYTES)


def _tile(n, pref, mult=128):
    if n <= pref:
        return n
    t = (pref // mult) * mult
    while t >= mult:
        if n % t == 0:
            return t
        t -= mult
    return n


def _sigmoid(x):
    return 0.5 + 0.5 * jnp.tanh(0.5 * x)


def _silu(x):
    h = 0.5 * x
    return h + h * jnp.tanh(h)


def _softplus(x):
    return jnp.maximum(x, 0.0) + jnp.log(1.0 + jnp.exp(-jnp.abs(x)))


def _wcols(K, tn, col0):
    assert col0 % LANES == 0 and tn % LANES == 0
    return pl.BlockSpec((pl.Element(K), pl.Element(tn)),
                        lambda j, i: (0, (col0 // LANES + j * (tn // LANES)) * LANES))


def _proj_act_kernel(a_ref, w_ref, o_ref, w16_ref, *, act):
    @pl.when(pl.program_id(1) == 0)
    def _():
        w16_ref[...] = w_ref[...].astype(BF16)

    acc = jnp.dot(a_ref[...], w16_ref[...], preferred_element_type=F32)
    if act == "silu":
        acc = _silu(acc)
    elif act == "sigmoid":
        acc = _sigmoid(acc)
    o_ref[...] = acc.astype(o_ref.dtype)


def _proj_act(a, w, col0, n, act, out_dtype, name):
    M, K = a.shape
    tm, tn = _tile(M, 512, 8), _tile(n, 1024)
    return pl.pallas_call(
        functools.partial(_proj_act_kernel, act=act),
        grid=(n // tn, M // tm),
        in_specs=[pl.BlockSpec((tm, K), lambda j, i: (i, 0)),
                  _wcols(K, tn, col0)],
        out_specs=pl.BlockSpec((tm, tn), lambda j, i: (i, j)),
        out_shape=jax.ShapeDtypeStruct((M, n), out_dtype),
        scratch_shapes=[pltpu.VMEM((K, tn), BF16)],
        compiler_params=_cparams(("parallel", "arbitrary")),
        name=name,
    )(a, w)


def _proj_glu_kernel(a_ref, wu_ref, wg_ref, bu_ref, bg_ref, o_ref, wu16_ref, wg16_ref):
    @pl.when(pl.program_id(1) == 0)
    def _():
        wu16_ref[...] = wu_ref[...].astype(BF16)
        wg16_ref[...] = wg_ref[...].astype(BF16)

    a = a_ref[...]
    u = jnp.dot(a, wu16_ref[...], preferred_element_type=F32) + bu_ref[...]
    g = jnp.dot(a, wg16_ref[...], preferred_element_type=F32) + bg_ref[...]
    o_ref[...] = (u * _sigmoid(g)).astype(o_ref.dtype)


def _proj_glu(a, w, col_u, col_g, n, bu, bg, out_dtype):
    M, K = a.shape
    tm, tn = _tile(M, 1024, 8), _tile(n, 256)
    return pl.pallas_call(
        _proj_glu_kernel,
        grid=(n // tn, M // tm),
        in_specs=[pl.BlockSpec((tm, K), lambda j, i: (i, 0)),
                  _wcols(K, tn, col_u),
                  _wcols(K, tn, col_g),
                  pl.BlockSpec((1, tn), lambda j, i: (0, j)),
                  pl.BlockSpec((1, tn), lambda j, i: (0, j))],
        out_specs=pl.BlockSpec((tm, tn), lambda j, i: (i, j)),
        out_shape=jax.ShapeDtypeStruct((M, n), out_dtype),
        scratch_shapes=[pltpu.VMEM((K, tn), BF16), pltpu.VMEM((K, tn), BF16)],
        compiler_params=_cparams(("parallel", "arbitrary")),
        name="proj_glu",
    )(a, w, w, bu.reshape(1, n), bg.reshape(1, n))


def _norm_proj_dt_kernel(x_ref, g_ref, w_ref, b_ref, h_ref, dt_ref, dtT_ref):
    x = x_ref[...]
    ms = jnp.mean(x * x, axis=-1, keepdims=True)
    h = (x * lax.rsqrt(ms + RMS_EPS) * g_ref[...]).astype(BF16)
    h_ref[...] = h
    acc = jnp.dot(h, w_ref[...].astype(BF16), preferred_element_type=F32) + b_ref[...]
    dt = _softplus(acc)
    dt_ref[...] = dt
    dtT_ref[...] = dt.T


def _norm_proj_dt(x, g, w, col0, H, b):
    M, K = x.shape
    tm = _tile(M, 256, 128)
    return pl.pallas_call(
        _norm_proj_dt_kernel,
        grid=(M // tm,),
        in_specs=[pl.BlockSpec((tm, K), lambda i: (i, 0)),
                  pl.BlockSpec((1, K), lambda i: (0, 0)),
                  pl.BlockSpec((pl.Element(K), pl.Element(H)), lambda i: (0, col0)),
                  pl.BlockSpec((1, H), lambda i: (0, 0))],
        out_specs=[pl.BlockSpec((tm, K), lambda i: (i, 0)),
                   pl.BlockSpec((tm, H), lambda i: (i, 0)),
                   pl.BlockSpec((H, tm), lambda i: (0, i))],
        out_shape=[jax.ShapeDtypeStruct((M, K), BF16),
                   jax.ShapeDtypeStruct((M, H), F32),
                   jax.ShapeDtypeStruct((H, M), F32)],
        compiler_params=_cparams(("parallel",)),
        name="norm_proj_dt",
    )(x, g.reshape(1, K), w, b.reshape(1, H))


def _out_gate_kernel(*refs, has_prev):
    if has_prev:
        a_ref, w_ref, g_ref, p_ref, o_ref = refs
    else:
        a_ref, w_ref, g_ref, o_ref = refs
    acc = jnp.dot(a_ref[...], w_ref[...], preferred_element_type=F32)
    acc = acc * g_ref[...].astype(F32)
    if has_prev:
        acc = acc + p_ref[...].astype(F32)
    o_ref[...] = acc.astype(o_ref.dtype)


def _out_gate(a, w, gates, gate_col0, prev, out_dtype, name):
    M, K = a.shape
    N = w.shape[1]
    tm, tn = _tile(M, (4 * 1024 * 1024) // K, 8), _tile(N, (4 * 1024 * 1024) // K)
    goff = gate_col0 // tn
    in_specs = [pl.BlockSpec((tm, K), lambda j, i: (i, 0)),
                pl.BlockSpec((K, tn), lambda j, i: (0, j)),
                pl.BlockSpec((tm, tn), lambda j, i: (i, j + goff))]
    args = [a, w, gates]
    if prev is not None:
        in_specs.append(pl.BlockSpec((tm, tn), lambda j, i: (i, j)))
        args.append(prev)
    return pl.pallas_call(
        functools.partial(_out_gate_kernel, has_prev=prev is not None),
        grid=(N // tn, M // tm),
        in_specs=in_specs,
        out_specs=pl.BlockSpec((tm, tn), lambda j, i: (i, j)),
        out_shape=jax.ShapeDtypeStruct((M, N), out_dtype),
        compiler_params=_cparams(("parallel", "parallel")),
        name=name,
    )(*args)


def _out_merge_kernel(a1_ref, w1_ref, a2_ref, w2_ref, g1_ref, g2_ref, o_ref):
    y1 = jnp.dot(a1_ref[...], w1_ref[...], preferred_element_type=F32)
    y2 = jnp.dot(a2_ref[...], w2_ref[...], preferred_element_type=F32)
    o_ref[...] = (y1 * g1_ref[...].astype(F32) + y2 * g2_ref[...].astype(F32)).astype(o_ref.dtype)


def _out_merge(a1, w1, a2, w2, gates, out_dtype):
    M, K1 = a1.shape
    K2 = a2.shape[1]
    N = w1.shape[1]
    tm, tn = _tile(M, 512, 8), _tile(N, 512)
    goff = N // tn
    return pl.pallas_call(
        _out_merge_kernel,
        grid=(N // tn, M // tm),
        in_specs=[pl.BlockSpec((tm, K1), lambda j, i: (i, 0)),
                  pl.BlockSpec((K1, tn), lambda j, i: (0, j)),
                  pl.BlockSpec((tm, K2), lambda j, i: (i, 0)),
                  pl.BlockSpec((K2, tn), lambda j, i: (0, j)),
                  pl.BlockSpec((tm, tn), lambda j, i: (i, j)),
                  pl.BlockSpec((tm, tn), lambda j, i: (i, j + goff))],
        out_specs=pl.BlockSpec((tm, tn), lambda j, i: (i, j)),
        out_shape=jax.ShapeDtypeStruct((M, N), out_dtype),
        compiler_params=_cparams(("parallel", "parallel")),
        name="out_merge",
    )(a1, w1, a2, w2, gates, gates)


def _out_res_kernel(a_ref, w_ref, r_ref, o_ref):
    acc = jnp.dot(a_ref[...], w_ref[...], preferred_element_type=F32)
    o_ref[...] = r_ref[...] + acc


def _out_res(a, w, res):
    M, K = a.shape
    N = w.shape[1]
    tm, tn = _tile(M, 1024, 8), _tile(N, 1024)
    return pl.pallas_call(
        _out_res_kernel,
        grid=(N // tn, M // tm),
        in_specs=[pl.BlockSpec((tm, K), lambda j, i: (i, 0)),
                  pl.BlockSpec((K, tn), lambda j, i: (0, j)),
                  pl.BlockSpec((tm, tn), lambda j, i: (i, j))],
        out_specs=pl.BlockSpec((tm, tn), lambda j, i: (i, j)),
        out_shape=jax.ShapeDtypeStruct((M, N), F32),
        compiler_params=_cparams(("parallel", "parallel")),
        name="out_res",
    )(a, w, res)


def _split3(v):
    hi = v.astype(BF16)
    r1 = v - hi.astype(F32)
    mid = r1.astype(BF16)
    lo = (r1 - mid.astype(F32)).astype(BF16)
    return hi, mid, lo


def _ssd_kernel(xs_ref, b_ref, c_ref, wx_ref, wb_ref, wc_ref, bx_ref, bb_ref, bc_ref,
                dt_ref, dtT_ref, alog_ref, alogT_ref, z_ref, dskip_ref, ng_ref,
                o_ref, ext_ref, state_ref, y_ref, *, Q, Hg, P, N, K):
    W = Hg * P
    HALO = 8
    NPL = Hg // 2
    c = pl.program_id(2)

    @pl.when(c == 0)
    def _():
        ext_ref[:, 0:HALO, :] = jnp.zeros((NPL + 2, HALO, 2 * P), F32)
        state_ref[...] = jnp.zeros_like(state_ref)

    for r0 in range(0, xs_ref.shape[0], Q):
        _ssd_chunk(xs_ref, b_ref, c_ref, wx_ref, wb_ref, wc_ref, bx_ref, bb_ref, bc_ref,
                   dt_ref, dtT_ref, alog_ref, alogT_ref, z_ref, dskip_ref, ng_ref,
                   o_ref, ext_ref, state_ref, y_ref, r0, Q=Q, Hg=Hg, P=P, N=N, K=K)


def _ssd_chunk(xs_ref, b_ref, c_ref, wx_ref, wb_ref, wc_ref, bx_ref, bb_ref, bc_ref,
               dt_ref, dtT_ref, alog_ref, alogT_ref, z_ref, dskip_ref, ng_ref,
               o_ref, ext_ref, state_ref, y_ref, r0, *, Q, Hg, P, N, K):
    W = Hg * P
    HALO = 8
    NPL = Hg // 2
    rows = slice(r0, r0 + Q)

    for p in range(NPL):
        ext_ref[p, HALO:HALO + Q, :] = xs_ref[rows, p * 2 * P:(p + 1) * 2 * P].astype(F32)
    ext_ref[NPL, HALO:HALO + Q, :] = b_ref[rows, :].astype(F32)
    ext_ref[NPL + 1, HALO:HALO + Q, :] = c_ref[rows, :].astype(F32)

    def conv_silu(plane, w_ref, bias_ref, wlo):
        acc = bias_ref[:, wlo:wlo + 2 * P]
        for k in range(K):
            r0 = HALO - (K - 1) + k
            acc = acc + w_ref[k:k + 1, wlo:wlo + 2 * P] * ext_ref[plane, r0:r0 + Q, :]
        return acc + acc * jnp.tanh(acc)

    bm = conv_silu(NPL, wb_ref, bb_ref, 0)
    cm = conv_silu(NPL + 1, wc_ref, bc_ref, 0)

    dt = dt_ref[rows, :]
    dtT = dtT_ref[:, rows]
    a = dt * (-jnp.exp(alog_ref[...]) * LOG2E)
    aT = dtT * (-jnp.exp(alogT_ref[...]) * LOG2E)

    row = lax.broadcasted_iota(jnp.int32, (Q, Q), 0)
    col = lax.broadcasted_iota(jnp.int32, (Q, Q), 1)
    causal = col <= row
    tril = jnp.where(causal, 1.0, 0.0).astype(BF16)
    triu = jnp.where(row <= col, 1.0, 0.0).astype(BF16)
    acum = jnp.dot(jnp.concatenate([tril, tril, tril], axis=1),
                   jnp.concatenate(_split3(a), axis=0), preferred_element_type=F32)
    acumT = jnp.dot(jnp.concatenate(_split3(aT), axis=1),
                    jnp.concatenate([triu, triu, triu], axis=0), preferred_element_type=F32)

    bm16 = bm.astype(BF16)
    cm16 = cm.astype(BF16)
    cb = lax.dot_general(cm16, bm16, (((1,), (1,)), ((), ())), preferred_element_type=F32)
    bT = bm.T

    lane = lax.broadcasted_iota(jnp.int32, (1, 2 * P), 1)
    lo_half = lane < P
    zero16 = jnp.zeros((), BF16)

    ssq = jnp.zeros((Q, 1), F32)
    for q in range(Hg // 2):
        l0 = q * 2 * P
        x_pair = conv_silu(q, wx_ref, bx_ref, l0)
        x16 = x_pair.astype(BF16)
        rhs_x = jnp.concatenate([jnp.where(lo_half, x16, zero16),
                                 jnp.where(lo_half, zero16, x16)], axis=0)
        s_old = state_ref[q]
        s16 = s_old.astype(BF16)
        rhs_s = jnp.concatenate([jnp.where(lo_half, s16, zero16),
                                 jnp.where(lo_half, zero16, s16)], axis=0)
        l_parts, ec_parts, bw_parts, g_parts = [], [], [], []
        for hh in range(2):
            h = 2 * q + hh
            ai = jnp.broadcast_to(acum[:, h:h + 1], (Q, Q))
            aj = acumT[h:h + 1, :]
            dtj = dtT[h:h + 1, :]
            seg = jnp.where(causal, ai - aj, NEG_BIG)
            l_parts.append((jnp.exp2(seg) * cb * dtj).astype(BF16))
            ec_parts.append((jnp.exp2(ai) * cm).astype(BF16))
            a_last = acumT[h:h + 1, Q - 1:Q]
            w_end = jnp.exp2(a_last - aj) * dtj
            bw_parts.append((bT * w_end).astype(BF16))
            g_parts.append(jnp.exp2(a_last))
        lhs_y = jnp.concatenate(l_parts + ec_parts, axis=1)
        rhs_y = jnp.concatenate([rhs_x, rhs_s], axis=0)
        y = jnp.dot(lhs_y, rhs_y, preferred_element_type=F32)
        upd = jnp.dot(jnp.concatenate(bw_parts, axis=1), rhs_x,
                      preferred_element_type=F32)
        decay = jnp.where(lo_half, g_parts[0], g_parts[1])
        state_ref[q] = s_old * decay + upd

        y = y + dskip_ref[:, l0:l0 + 2 * P] * x_pair
        y = y * z_ref[rows, l0:l0 + 2 * P].astype(F32)
        y_ref[:, l0:l0 + 2 * P] = y
        ssq = ssq + jnp.sum(y * y, axis=-1, keepdims=True)

    ext_ref[:, 0:HALO, :] = ext_ref[:, Q:Q + HALO, :]
    rstd = lax.rsqrt(ssq * (1.0 / W) + RMS_EPS)
    o_ref[rows, :] = (y_ref[...] * rstd * ng_ref[...]).astype(o_ref.dtype)


def _ssd(xbc, zs, dt, dtT, conv_w, conv_b, a_log, d_skip, norm_g, *, batch, seq, d_ssd):
    T = batch * seq
    G = SSD_GROUPS
    Q = SSD_CHUNK
    H = a_log.shape[0]
    Hg = H // G
    P = d_ssd // H
    W = Hg * P
    N = (xbc.shape[1] - d_ssd) // (2 * G)
    K = conv_w.shape[0]
    assert 2 * P == 128 and N == 128 and Q == 128 and Hg % 2 == 0 and W % 128 == 0
    assert seq % Q == 0 and K - 1 <= 8
    QB = Q * SSD_CHUNKS_PER_STEP if seq % (Q * SSD_CHUNKS_PER_STEP) == 0 else Q
    nc = seq // QB
    nb_x = d_ssd // N

    dt_g = dt.reshape(T, G, Hg).transpose(1, 0, 2)
    alog_g = a_log.reshape(G, 1, Hg)
    alogT = a_log.reshape(H, 1)
    dskip = jnp.repeat(d_skip, P).reshape(1, d_ssd)
    conv_w = 0.5 * conv_w
    cb2 = (0.5 * conv_b).reshape(1, -1)

    rowblk = lambda b, g, c: b * nc + c
    kern = functools.partial(_ssd_kernel, Q=Q, Hg=Hg, P=P, N=N, K=K)
    return pl.pallas_call(
        kern,
        grid=(batch, G, nc),
        in_specs=[
            pl.BlockSpec((QB, W), lambda b, g, c: (rowblk(b, g, c), g)),
            pl.BlockSpec((QB, N), lambda b, g, c: (rowblk(b, g, c), nb_x + g)),
            pl.BlockSpec((QB, N), lambda b, g, c: (rowblk(b, g, c), nb_x + G + g)),
            pl.BlockSpec((K, W), lambda b, g, c: (0, g)),
            pl.BlockSpec((K, N), lambda b, g, c: (0, nb_x + g)),
            pl.BlockSpec((K, N), lambda b, g, c: (0, nb_x + G + g)),
            pl.BlockSpec((1, W), lambda b, g, c: (0, g)),
            pl.BlockSpec((1, N), lambda b, g, c: (0, nb_x + g)),
            pl.BlockSpec((1, N), lambda b, g, c: (0, nb_x + G + g)),
            pl.BlockSpec((None, QB, Hg), lambda b, g, c: (g, rowblk(b, g, c), 0)),
            pl.BlockSpec((Hg, QB), lambda b, g, c: (g, rowblk(b, g, c))),
            pl.BlockSpec((None, 1, Hg), lambda b, g, c: (g, 0, 0)),
            pl.BlockSpec((Hg, 1), lambda b, g, c: (g, 0)),
            pl.BlockSpec((QB, W), lambda b, g, c: (rowblk(b, g, c), g)),
            pl.BlockSpec((1, W), lambda b, g, c: (0, g)),
            pl.BlockSpec((1, W), lambda b, g, c: (0, g)),
        ],
        out_specs=pl.BlockSpec((QB, W), lambda b, g, c: (rowblk(b, g, c), g)),
        out_shape=jax.ShapeDtypeStruct((T, d_ssd), BF16),
        scratch_shapes=[pltpu.VMEM((Hg // 2 + 2, Q + 8, 2 * P), F32),
                        pltpu.VMEM((Hg // 2, N, 2 * P), F32),
                        pltpu.VMEM((Q, W), F32)],
        compiler_params=_cparams(("parallel", "parallel", "arbitrary")),
        name="ssd_scan",
    )(xbc, xbc, xbc, conv_w, conv_w, conv_w, cb2, cb2, cb2,
      dt_g, dtT, alog_g, alogT, zs, dskip, norm_g.reshape(1, d_ssd))


def _conf_kernel(u_ref, w_ref, b_ref, g_ref, beta_ref, o_ref, ext_ref, conv_ref, *, TM, C, K, HALO):
    NCH = C // 128
    i = pl.program_id(1)

    @pl.when(i == 0)
    def _():
        ext_ref[:, 0:HALO, :] = jnp.zeros((NCH, HALO, 128), F32)

    for ch in range(NCH):
        ext_ref[ch, HALO:HALO + TM, :] = u_ref[:, ch * 128:(ch + 1) * 128].astype(F32)

    def chunk_body(ch, carry):
        s1, = carry
        acc = jnp.broadcast_to(b_ref[ch], (TM, 128))
        for k in range(K):
            r0 = HALO - (K - 1) + k
            acc = acc + w_ref[ch, k:k + 1, :] * ext_ref[ch, r0:r0 + TM, :]
        conv_ref[ch] = acc
        ext_ref[ch, 0:HALO, :] = ext_ref[ch, TM:TM + HALO, :]
        return (s1 + acc,)

    s1, = lax.fori_loop(0, NCH, chunk_body, (jnp.zeros((TM, 128), F32),))
    mu = jnp.sum(s1, axis=-1, keepdims=True) * (1.0 / C)

    def var_body(ch, s2):
        d = conv_ref[ch] - mu
        return s2 + d * d

    s2 = lax.fori_loop(0, NCH, var_body, jnp.zeros((TM, 128), F32))
    rstd = lax.rsqrt(jnp.sum(s2, axis=-1, keepdims=True) * (1.0 / C) + LN_EPS)
    for ch in range(NCH):
        h = (conv_ref[ch] - mu) * rstd * g_ref[ch] + beta_ref[ch]
        o_ref[:, ch * 128:(ch + 1) * 128] = (h + h * jnp.tanh(h)).astype(o_ref.dtype)


def _conformer(u, dw_w, dw_b, ln_g, ln_b, *, batch, seq):
    T, C = u.shape
    K = dw_w.shape[0]
    HALO = 32
    assert K - 1 <= HALO and C % 128 == 0
    TM = _tile(seq, 256, 32)
    nt = seq // TM
    NCH = C // 128
    w3 = dw_w.reshape(K, NCH, 128).transpose(1, 0, 2)
    chunked = lambda v: v.reshape(NCH, 1, 128)
    kern = functools.partial(_conf_kernel, TM=TM, C=C, K=K, HALO=HALO)
    return pl.pallas_call(
        kern,
        grid=(batch, nt),
        in_specs=[pl.BlockSpec((TM, C), lambda b, i: (b * nt + i, 0)),
                  pl.BlockSpec((NCH, K, 128), lambda b, i: (0, 0, 0)),
                  pl.BlockSpec((NCH, 1, 128), lambda b, i: (0, 0, 0)),
                  pl.BlockSpec((NCH, 1, 128), lambda b, i: (0, 0, 0)),
                  pl.BlockSpec((NCH, 1, 128), lambda b, i: (0, 0, 0))],
        out_specs=pl.BlockSpec((TM, C), lambda b, i: (b * nt + i, 0)),
        out_shape=jax.ShapeDtypeStruct((T, C), BF16),
        scratch_shapes=[pltpu.VMEM((NCH, HALO + TM, 128), F32),
                        pltpu.VMEM((NCH, TM, 128), F32)],
        compiler_params=_cparams(("parallel", "arbitrary")),
        name="conformer",
    )(u, w3, chunked(dw_b), chunked(0.5 * ln_g), chunked(0.5 * ln_b))


def _pack_halves(v):
    d2 = v.shape[1] // 2
    hi = lax.bitcast_convert_type(v[:, :d2].astype(BF16).astype(F32), jnp.uint32)
    lo = lax.bitcast_convert_type(v[:, d2:].astype(BF16).astype(F32), jnp.uint32)
    return hi | (lo >> 16)


def _unpack_halves(p):
    hi = lax.bitcast_convert_type(p & jnp.uint32(0xFFFF0000), F32)
    lo = lax.bitcast_convert_type(p << 16, F32)
    return hi, lo


def _router_kernel(x_ref, g_ref, w2_ref, whi_ref, b_ref, h_ref, idx_ref, cw_ref, cnt_ref, *, TM, NG, EPG):
    i = pl.program_id(0)

    @pl.when(i == 0)
    def _():
        cnt_ref[...] = jnp.zeros_like(cnt_ref)

    x = x_ref[...]
    ms = jnp.mean(x * x, axis=-1, keepdims=True)
    h = x * lax.rsqrt(ms + RMS_EPS) * g_ref[...]
    h_ref[...] = _pack_halves(h)

    h_hi = h.astype(BF16)
    h_lo = (h - h_hi.astype(F32)).astype(BF16)
    p2 = jnp.dot(h_hi, w2_ref[...], preferred_element_type=F32)
    p1 = jnp.dot(h_lo, whi_ref[...], preferred_element_type=F32)
    L = ROUTER_LANES
    lg = p2[:, 0:L] + p2[:, L:2 * L] + p1 + b_ref[...]

    lane_i = lax.broadcasted_iota(jnp.int32, (TM, L), 1)
    lane = lane_i.astype(F32)
    gl = jnp.where(lane_i < NG, lg, NEG_BIG)
    gmax = jnp.max(gl, axis=-1, keepdims=True)
    gidx = jnp.min(jnp.where(gl == gmax, lane, float(L)), axis=-1, keepdims=True)
    p_top = 1.0 / jnp.sum(jnp.exp(gl - gmax), axis=-1, keepdims=True)

    egrp = jnp.floor((lane - EXPERT_LANE0) * (1.0 / EPG))
    sel = jnp.logical_and(lane_i >= EXPERT_LANE0, egrp == gidx)
    el = jnp.where(sel, lg, NEG_BIG)
    v0 = jnp.max(el, axis=-1, keepdims=True)
    i0 = jnp.min(jnp.where(el == v0, lane, float(L)), axis=-1, keepdims=True)
    el1 = jnp.where(lane == i0, NEG_BIG, el)
    v1 = jnp.max(el1, axis=-1, keepdims=True)
    i1 = jnp.min(jnp.where(el1 == v1, lane, float(L)), axis=-1, keepdims=True)
    t = jnp.exp(v1 - v0)
    w0 = 1.0 / (1.0 + t)
    c0 = p_top * w0
    c1 = p_top * (t * w0)

    oh0 = lane == i0
    oh1 = lane == i1
    oh = jnp.where(jnp.logical_or(oh0, oh1), 1.0, 0.0)
    r = lax.broadcasted_iota(jnp.int32, (TM, TM), 0)
    cidx = lax.broadcasted_iota(jnp.int32, (TM, TM), 1)
    strict = jnp.where(cidx < r, 1.0, 0.0).astype(BF16)
    before = jnp.dot(strict, oh.astype(BF16), preferred_element_type=F32) + cnt_ref[...]
    rank0 = jnp.sum(jnp.where(oh0, before, 0.0), axis=-1, keepdims=True)
    rank1 = jnp.sum(jnp.where(oh1, before, 0.0), axis=-1, keepdims=True)
    cnt_ref[...] = cnt_ref[...] + jnp.sum(oh, axis=0, keepdims=True)

    e0 = i0 - EXPERT_LANE0
    e1 = i1 - EXPERT_LANE0
    idx = jnp.where(lane_i == 0, e0, jnp.where(lane_i == 1, e1,
          jnp.where(lane_i == 2, rank0, jnp.where(lane_i == 3, rank1, 0.0))))
    idx_ref[...] = idx.astype(jnp.int32)
    cw_ref[...] = jnp.where(lane_i == 0, c0, jnp.where(lane_i == 1, c1, 0.0))


def _router(x1, g, r_grp, r_grp_b, r_exp, r_exp_b):
    T, D = x1.shape
    NG = r_grp.shape[1]
    EPG = r_exp.shape[2]
    NE = NG * EPG
    L = ROUTER_LANES
    assert NG <= EXPERT_LANE0 and EXPERT_LANE0 + NE <= L
    w = jnp.zeros((D, L), F32)
    w = w.at[:, 0:NG].set(r_grp).at[:, EXPERT_LANE0:EXPERT_LANE0 + NE].set(r_exp.reshape(D, NE))
    b = jnp.zeros((1, L), F32)
    b = b.at[0, 0:NG].set(r_grp_b).at[0, EXPERT_LANE0:EXPERT_LANE0 + NE].set(r_exp_b.reshape(NE))
    w_hi = w.astype(BF16)
    w_lo = (w - w_hi.astype(F32)).astype(BF16)
    w2 = jnp.concatenate([w_hi, w_lo], axis=1)
    TM = _tile(T, 256, 8)
    kern = functools.partial(_router_kernel, TM=TM, NG=NG, EPG=EPG)
    return pl.pallas_call(
        kern,
        grid=(T // TM,),
        in_specs=[pl.BlockSpec((TM, D), lambda i: (i, 0)),
                  pl.BlockSpec((1, D), lambda i: (0, 0)),
                  pl.BlockSpec((D, 2 * L), lambda i: (0, 0)),
                  pl.BlockSpec((D, L), lambda i: (0, 0)),
                  pl.BlockSpec((1, L), lambda i: (0, 0))],
        out_specs=[pl.BlockSpec((TM, D // 2), lambda i: (i, 0)),
                   pl.BlockSpec((TM, L), lambda i: (i, 0)),
                   pl.BlockSpec((TM, L), lambda i: (i, 0)),
                   pl.BlockSpec((1, L), lambda i: (0, 0))],
        out_shape=[jax.ShapeDtypeStruct((T, D // 2), jnp.uint32),
                   jax.ShapeDtypeStruct((T, L), jnp.int32),
                   jax.ShapeDtypeStruct((T, L), F32),
                   jax.ShapeDtypeStruct((1, L), F32)],
        compiler_params=_cparams(("arbitrary",)),
        name="router",
    )(x1, g.reshape(1, D), w2, w_hi, b)


def _positions_kernel(idx_ref, starts_ref, pos_ref):
    idx = idx_ref[...]
    lane = lax.broadcasted_iota(jnp.int32, idx.shape, 1)
    starts = starts_ref[...]

    def lookup(e):
        return jnp.sum(jnp.where(lane == e + EXPERT_LANE0, starts, 0.0), axis=-1, keepdims=True)

    pos0 = lookup(idx[:, 0:1]).astype(jnp.int32) + idx[:, 2:3]
    pos1 = lookup(idx[:, 1:2]).astype(jnp.int32) + idx[:, 3:4]
    pos_ref[...] = jnp.where(lane == 0, pos0, jnp.where(lane == 1, pos1, 0))


def _positions(idx, starts_row):
    T, L = idx.shape
    TM = _tile(T, 1024, 8)
    return pl.pallas_call(
        _positions_kernel,
        grid=(T // TM,),
        in_specs=[pl.BlockSpec((TM, L), lambda i: (i, 0)),
                  pl.BlockSpec((1, L), lambda i: (0, 0))],
        out_specs=pl.BlockSpec((TM, L), lambda i: (i, 0)),
        out_shape=jax.ShapeDtypeStruct((T, L), jnp.int32),
        compiler_params=_cparams(("parallel",)),
        name="moe_positions",
    )(idx, starts_row)


def _dispatch_kernel(pos0_ref, pos1_ref, ptile_ref, nu_ref, h_ref, xs_ref, zbuf, sem, zsem, *, TM, TME, NE, NT):
    base = pl.program_id(0) * TM

    @pl.when(pl.program_id(0) == 0)
    def _():
        zbuf[...] = jnp.zeros_like(zbuf)

        def zero_tile(t):
            return pltpu.make_async_copy(zbuf, xs_ref.at[pl.ds(pl.multiple_of(t * TME, TME), TME)], zsem)

        def expert_tiles(op):
            def body(e, carry):
                @pl.when(ptile_ref[e] >= 0)
                def _():
                    op(zero_tile(ptile_ref[e]))
                return carry
            lax.fori_loop(0, NE, body, 0)

        def tail_tiles(op):
            def body(t, carry):
                op(zero_tile(t))
                return carry
            lax.fori_loop(nu_ref[0], NT, body, 0)

        expert_tiles(lambda cp: cp.start())
        tail_tiles(lambda cp: cp.start())
        expert_tiles(lambda cp: cp.wait())
        tail_tiles(lambda cp: cp.wait())

    def copies(r):
        src = h_ref.at[pl.ds(r, 1)]
        return (pltpu.make_async_copy(src, xs_ref.at[pl.ds(pos0_ref[base + r], 1)], sem.at[0]),
                pltpu.make_async_copy(src, xs_ref.at[pl.ds(pos1_ref[base + r], 1)], sem.at[1]))

    def start(r, carry):
        for cp in copies(r):
            cp.start()
        return carry

    def wait(r, carry):
        for cp in copies(r):
            cp.wait()
        return carry

    lax.fori_loop(0, TM, start, 0, unroll=DMA_UNROLL)
    lax.fori_loop(0, TM, wait, 0, unroll=DMA_UNROLL)


def _dispatch(h2, pos0, pos1, pad_tile, n_used, rows, tme):
    T, D2 = h2.shape
    TM = _tile(T, 256, 8)
    grid_spec = pltpu.PrefetchScalarGridSpec(
        num_scalar_prefetch=4,
        grid=(T // TM,),
        in_specs=[pl.BlockSpec((TM, D2), lambda i, *_: (i, 0))],
        out_specs=pl.BlockSpec(memory_space=pl.ANY),
        scratch_shapes=[pltpu.VMEM((tme, D2), h2.dtype),
                        pltpu.SemaphoreType.DMA((2,)),
                        pltpu.SemaphoreType.DMA(())],
    )
    kern = functools.partial(_dispatch_kernel, TM=TM, TME=tme, NE=pad_tile.shape[0], NT=rows // tme)
    return pl.pallas_call(
        kern,
        grid_spec=grid_spec,
        out_shape=jax.ShapeDtypeStruct((rows, D2), h2.dtype),
        compiler_params=_cparams(("arbitrary",)),
        name="moe_dispatch",
    )(pos0, pos1, pad_tile, n_used, h2)


def _expert_weight_copies(hbm_refs, stage_refs, sem, e, s):
    out = []
    for m, (h, st) in enumerate(zip(hbm_refs, stage_refs)):
        half = h.shape[1] // 2
        for p in range(2):
            rows = pl.ds(p * half, half)
            out.append((pltpu.make_async_copy(h.at[e, rows], st.at[s, rows], sem.at[s, 2 * m + p]), p))
    return out


def _stage_expert_weights(te_ref, first_ref, slot_ref, next_ref, hbm_refs, stage_refs, w16_refs, sem):
    i = pl.program_id(0)

    @pl.when(i == 0)
    def _():
        for cp, prio in _expert_weight_copies(hbm_refs, stage_refs, sem, te_ref[0], 0):
            cp.start(priority=prio)

    @pl.when(first_ref[i] == 1)
    def _():
        s = slot_ref[i]

        @pl.when(next_ref[i] >= 0)
        def _():
            for cp, prio in _expert_weight_copies(hbm_refs, stage_refs, sem, next_ref[i], 1 - s):
                cp.start(priority=prio)

        for cp, _ in _expert_weight_copies(hbm_refs, stage_refs, sem, te_ref[i], s):
            cp.wait()
        for st, w16 in zip(stage_refs, w16_refs):
            w16[...] = st[s].astype(BF16)


def _experts_up_kernel(te_ref, ts_ref, first_ref, slot_ref, next_ref, nu_ref,
                       x_ref, wg_hbm, wu_hbm, hid_ref, wg_st, wu_st, wg16_ref, wu16_ref, sem):
    del ts_ref
    i = pl.program_id(0)
    _stage_expert_weights(te_ref, first_ref, slot_ref, next_ref,
                          (wg_hbm, wu_hbm), (wg_st, wu_st), (wg16_ref, wu16_ref), sem)

    @pl.when(i < nu_ref[0])
    def _():
        hi, lo = _unpack_halves(x_ref[...])
        x = jnp.concatenate([hi.astype(BF16), lo.astype(BF16)], axis=1)
        g = jnp.dot(x, wg16_ref[...], preferred_element_type=F32)
        u = jnp.dot(x, wu16_ref[...], preferred_element_type=F32)
        hid_ref[...] = (_silu(g) * u).astype(hid_ref.dtype)

    @pl.when(i >= nu_ref[0])
    def _():
        hid_ref[...] = jnp.zeros_like(hid_ref)


def _experts_up(xs, w_gate, w_up, plan, tme):
    R, D2 = xs.shape
    NE, D, F = w_gate.shape
    nt = R // tme
    nsp = len(plan)
    grid_spec = pltpu.PrefetchScalarGridSpec(
        num_scalar_prefetch=nsp,
        grid=(nt,),
        in_specs=[pl.BlockSpec((tme, D2), lambda i, te, ts, *_: (ts[i], 0)),
                  pl.BlockSpec(memory_space=pl.ANY),
                  pl.BlockSpec(memory_space=pl.ANY)],
        out_specs=pl.BlockSpec((tme, F), lambda i, *_: (i, 0)),
        scratch_shapes=[pltpu.VMEM((2, D, F), F32), pltpu.VMEM((2, D, F), F32),
                        pltpu.VMEM((D, F), BF16), pltpu.VMEM((D, F), BF16),
                        pltpu.SemaphoreType.DMA((2, 4))],
    )
    return pl.pallas_call(
        _experts_up_kernel,
        grid_spec=grid_spec,
        out_shape=jax.ShapeDtypeStruct((R, F), BF16),
        compiler_params=_cparams(("arbitrary",)),
        name="moe_experts_up",
    )(*plan, xs, w_gate, w_up)


def _experts_down_kernel(te_ref, ts_ref, first_ref, slot_ref, next_ref, nu_ref,
                         hid_ref, wd_hbm, y_ref, wd_st, wd16_ref, sem):
    del ts_ref
    i = pl.program_id(0)
    _stage_expert_weights(te_ref, first_ref, slot_ref, next_ref, (wd_hbm,), (wd_st,), (wd16_ref,), sem)

    @pl.when(i < nu_ref[0])
    def _():
        y = jnp.dot(hid_ref[...], wd16_ref[...], preferred_element_type=F32)
        y_ref[...] = _pack_halves(y)

    @pl.when(i >= nu_ref[0])
    def _():
        y_ref[...] = jnp.zeros_like(y_ref)


def _experts_down(hid, w_down, plan, tme):
    R, F = hid.shape
    NE, _, D = w_down.shape
    nt = R // tme
    grid_spec = pltpu.PrefetchScalarGridSpec(
        num_scalar_prefetch=len(plan),
        grid=(nt,),
        in_specs=[pl.BlockSpec((tme, F), lambda i, *_: (i, 0)),
                  pl.BlockSpec(memory_space=pl.ANY)],
        out_specs=pl.BlockSpec((tme, D // 2), lambda i, *_: (i, 0)),
        scratch_shapes=[pltpu.VMEM((2, F, D), F32), pltpu.VMEM((F, D), BF16),
                        pltpu.SemaphoreType.DMA((2, 2))],
    )
    return pl.pallas_call(
        _experts_down_kernel,
        grid_spec=grid_spec,
        out_shape=jax.ShapeDtypeStruct((R, D // 2), jnp.uint32),
        compiler_params=_cparams(("arbitrary",)),
        name="moe_experts_down",
    )(*plan, hid, w_down)


def _combine_kernel(pos0_ref, pos1_ref, x_ref, cw_ref, g_ref, y_ref, o_ref, buf, sem, *, TM, final_norm):
    i = pl.program_id(0)
    n = pl.num_programs(0)
    D2 = x_ref.shape[1] // 2

    def copies(step, slot, r):
        t = step * TM + r
        return (pltpu.make_async_copy(y_ref.at[pl.ds(pos0_ref[t], 1)], buf.at[slot, 0, pl.ds(r, 1)], sem.at[slot, 0]),
                pltpu.make_async_copy(y_ref.at[pl.ds(pos1_ref[t], 1)], buf.at[slot, 1, pl.ds(r, 1)], sem.at[slot, 1]))

    def start_all(step, slot):
        def body(r, carry):
            for cp in copies(step, slot, r):
                cp.start()
            return carry
        lax.fori_loop(0, TM, body, 0, unroll=DMA_UNROLL)

    def wait_all(step, slot):
        def body(r, carry):
            for cp in copies(step, slot, r):
                cp.wait()
            return carry
        lax.fori_loop(0, TM, body, 0, unroll=DMA_UNROLL)

    slot = lax.rem(i, COMBINE_SLOTS)
    slot_ahead = lax.rem(i + 2, COMBINE_SLOTS)

    @pl.when(i == 0)
    def _():
        start_all(0, 0)

        @pl.when(n > 1)
        def _():
            start_all(1, 1)

    wait_all(i, slot)

    RB = 8

    def rows_body(issue_ahead):
        def body(rb, carry):
            if issue_ahead:
                for r in range(RB):
                    for cp in copies(i + 2, slot_ahead, rb * RB + r):
                        cp.start()
            rows = pl.ds(pl.multiple_of(rb * RB, RB), RB)
            cw = cw_ref[rows, :]
            c0 = cw[:, 0:1]
            c1 = cw[:, 1:2]
            y0h, y0l = _unpack_halves(buf[slot, 0, rows, :])
            y1h, y1l = _unpack_halves(buf[slot, 1, rows, :])
            xh = x_ref[rows, :D2] + c0 * y0h + c1 * y1h
            xl = x_ref[rows, D2:] + c0 * y0l + c1 * y1l
            if final_norm:
                ssq = jnp.sum(xh * xh, axis=-1, keepdims=True) + jnp.sum(xl * xl, axis=-1, keepdims=True)
                rstd = lax.rsqrt(ssq * (1.0 / (2 * D2)) + RMS_EPS)
                xh = xh * rstd * g_ref[:, :D2]
                xl = xl * rstd * g_ref[:, D2:]
            o_ref[rows, :D2] = xh
            o_ref[rows, D2:] = xl
            return carry
        return body

    @pl.when(i + 2 < n)
    def _():
        lax.fori_loop(0, TM // RB, rows_body(True), 0, unroll=4)

    @pl.when(i + 2 >= n)
    def _():
        lax.fori_loop(0, TM // RB, rows_body(False), 0, unroll=4)


def _combine(x1, cw, y, pos0, pos1, g, final_norm):
    T, D = x1.shape
    TM = _tile(T, 256, 8)
    L = cw.shape[1]
    grid_spec = pltpu.PrefetchScalarGridSpec(
        num_scalar_prefetch=2,
        grid=(T // TM,),
        in_specs=[pl.BlockSpec((TM, D), lambda i, p0, p1: (i, 0)),
                  pl.BlockSpec((TM, L), lambda i, p0, p1: (i, 0)),
                  pl.BlockSpec((1, D), lambda i, p0, p1: (0, 0)),
                  pl.BlockSpec(memory_space=pl.ANY)],
        out_specs=pl.BlockSpec((TM, D), lambda i, p0, p1: (i, 0)),
        scratch_shapes=[pltpu.VMEM((COMBINE_SLOTS, 2, TM, D // 2), jnp.uint32),
                        pltpu.SemaphoreType.DMA((COMBINE_SLOTS, 2))],
    )
    return pl.pallas_call(
        functools.partial(_combine_kernel, TM=TM, final_norm=final_norm),
        grid_spec=grid_spec,
        out_shape=jax.ShapeDtypeStruct((T, D), F32),
        compiler_params=_cparams(("arbitrary",)),
        name="moe_combine",
    )(pos0, pos1, x1, cw, g.reshape(1, D), y)


def _moe(x1, norm_g, r_grp, r_grp_b, r_exp, r_exp_b, w_gate, w_up, w_down, final_g, final_norm):
    T, D = x1.shape
    NG, EPG = r_exp.shape[1], r_exp.shape[2]
    NE = NG * EPG
    F = w_gate.shape[-1]
    tme = 256
    h2, idx, cw, cnt = _router(x1, norm_g, r_grp, r_grp_b, r_exp, r_exp_b)

    counts = cnt[0, EXPERT_LANE0:EXPERT_LANE0 + NE].astype(jnp.int32)
    padded = ((counts + tme - 1) // tme) * tme
    ends = jnp.cumsum(padded)
    starts = ends - padded
    nt = (2 * T + NE * (tme - 1) + tme - 1) // tme
    n_used = (ends[-1] // tme).astype(jnp.int32)
    tile_ids = jnp.minimum(jnp.arange(nt, dtype=jnp.int32), n_used - 1)
    tile_expert = jnp.minimum(
        jnp.sum((ends[None, :] <= (tile_ids * tme)[:, None]).astype(jnp.int32), axis=1), NE - 1)
    starts_row = jnp.zeros((1, ROUTER_LANES), F32).at[0, EXPERT_LANE0:EXPERT_LANE0 + NE].set(starts.astype(F32))
    ids = jnp.arange(NE, dtype=jnp.int32)
    later = jnp.logical_and(ids[None, :] > ids[:, None], (counts > 0)[None, :])
    next_expert = jnp.min(jnp.where(later, ids[None, :], NE), axis=1)
    next_expert = jnp.where(next_expert == NE, -1, next_expert).astype(jnp.int32)
    prev_expert = jnp.concatenate([jnp.full((1,), -1, jnp.int32), tile_expert[:-1]])
    first = jnp.logical_and(tile_expert != prev_expert, jnp.arange(nt) < n_used).astype(jnp.int32)
    slot = ((jnp.cumsum(first) - 1) % 2).astype(jnp.int32)
    plan = (tile_expert, tile_ids, first, slot, next_expert[tile_expert], n_used.reshape(1))

    pos = _positions(idx, starts_row)
    pos0, pos1 = pos[:, 0], pos[:, 1]
    pad_tile = jnp.where(padded > 0, ends // tme - 1, -1).astype(jnp.int32)
    xs = _dispatch(h2, pos0, pos1, pad_tile, n_used.reshape(1), nt * tme, tme)
    hid = _experts_up(xs, w_gate.reshape(NE, D, F), w_up.reshape(NE, D, F), plan, tme)
    y = _experts_down(hid, w_down.reshape(NE, F, D), plan, tme)
    return _combine(x1, cw, y, pos0, pos1, final_g, final_norm)


def kernel(x, norm_mix, w_in, ssd_conv_w, ssd_conv_b, ssd_dt_bias, ssd_a_log, ssd_d, ssd_norm, ssd_w_out, conf_glu_b, conf_dw_w, conf_dw_b, conf_ln_g, conf_ln_b, conf_w_out, w_out, norm_ffn, router_group, router_group_b, router_expert, router_expert_b, expert_w_gate, expert_w_up, expert_w_down, norm_final):
    B, S, D = x.shape
    T = B * S
    depth = w_in.shape[0]
    d_ssd = ssd_norm.shape[1]
    d_xbc = ssd_conv_w.shape[2]
    H = ssd_a_log.shape[1]
    d_conf = conf_dw_w.shape[2]
    o_xbc = d_ssd
    o_dt = o_xbc + d_xbc
    o_glu = o_dt + H
    o_gate = o_glu + 2 * d_conf

    xf = x.reshape(T, D)
    for l in range(depth):
        wl = w_in[l]
        h, dt, dtT = _norm_proj_dt(xf, norm_mix[l], wl, o_dt, H, ssd_dt_bias[l])
        zs = _proj_act(h, wl, 0, d_ssd, "silu", BF16, "proj_z")
        xbc = _proj_act(h, wl, o_xbc, d_xbc, "none", BF16, "proj_xbc")
        glu = _proj_glu(h, wl, o_glu, o_glu + d_conf, d_conf,
                        conf_glu_b[l, :d_conf], conf_glu_b[l, d_conf:], BF16)
        gates = _proj_act(h, wl, o_gate, 2 * D, "sigmoid", BF16, "proj_gates")

        yn = _ssd(xbc, zs, dt, dtT, ssd_conv_w[l], ssd_conv_b[l], ssd_a_log[l], ssd_d[l], ssd_norm[l],
                  batch=B, seq=S, d_ssd=d_ssd)
        uc = _conformer(glu, conf_dw_w[l], conf_dw_b[l], conf_ln_g[l], conf_ln_b[l], batch=B, seq=S)

        mixed = _out_merge(yn, ssd_w_out[l].astype(BF16), uc, conf_w_out[l].astype(BF16), gates, BF16)
        x1 = _out_res(mixed, w_out[l].astype(BF16), xf)

        xf = _moe(x1, norm_ffn[l], router_group[l], router_group_b[l], router_expert[l],
                  router_expert_b[l], expert_w_gate[l], expert_w_up[l], expert_w_down[l],
                  norm_final, l == depth - 1)
    return xf.reshape(B, S, D)
```
